```python
import math
import jax, jax.numpy as jnp
from jax import lax
import numpy as np

D_MODEL = 1024
BATCH = 16
SEQ = 2048
DEPTH = 1

N_MEM = 256
DA_HEADS = 4
DA_HEAD_DIM = 64
DA_V_DIM = 2 * DA_HEAD_DIM
DA_WIDTH = DA_HEADS * 2 * DA_HEAD_DIM
ROPE_THETA = 10000.0
Q_BLOCK = 128
SSM_WIDTH = 512
SSM_GROUP = 16
SSM_GROUPS = SSM_WIDTH // SSM_GROUP
SSM_STATE = 64
X_HEADS = 4
X_HEAD_DIM = D_MODEL // X_HEADS
N_EXPERTS = 32
TOP_K = 4
D_FF = D_MODEL
SWIGLU_LIMIT = 7.0
SWIGLU_ALPHA = 1.702
MOE_BLOCK = 256
EPS = 1e-6
IN_WIDTH = 3 * DA_WIDTH + SSM_WIDTH + 2 * D_MODEL
IN_SPLITS = (DA_WIDTH, 2 * DA_WIDTH, 3 * DA_WIDTH, 3 * DA_WIDTH + SSM_WIDTH, 3 * DA_WIDTH + SSM_WIDTH + D_MODEL)

kernel_name = 'hybrid_diffattn_s5_moe_encoder_layer'


def rms_norm(x, g):
    xf = x.astype(jnp.float32)
    y = xf * lax.rsqrt(jnp.mean(xf * xf, axis=-1, keepdims=True) + EPS)
    return (y * g.astype(jnp.float32)).astype(x.dtype)


def rope(x, pos):
    dh = x.shape[-1]
    half = dh // 2
    inv = ROPE_THETA ** (-jnp.arange(half, dtype=jnp.float32) * (2.0 / dh))
    ang = pos.astype(jnp.float32)[:, None] * inv[None, :]
    shape = (1, pos.shape[0]) + (1,) * (x.ndim - 3) + (half,)
    cos = jnp.cos(ang).reshape(shape)
    sin = jnp.sin(ang).reshape(shape)
    xf = x.astype(jnp.float32)
    x1, x2 = xf[..., :half], xf[..., half:]
    return jnp.concatenate([x1 * cos - x2 * sin, x2 * cos + x1 * sin], axis=-1).astype(x.dtype)


def diff_attention(q, k, v, lam, subln_g, lambda_init):
    b, l = q.shape[:2]
    nb = l // Q_BLOCK
    scale = DA_HEAD_DIM ** -0.5
    qb = q.reshape(b, nb, Q_BLOCK, DA_HEADS, 2, DA_HEAD_DIM).swapaxes(0, 1)

    def one_block(qi):
        s = jnp.einsum('bqhcd,bkhcd->bhcqk', qi, k, preferred_element_type=jnp.float32) * scale
        p = jax.nn.softmax(s, axis=-1)
        w = p[:, :, 0] - lam * p[:, :, 1]
        return jnp.einsum('bhqk,bkhe->bqhe', w.astype(v.dtype), v)

    o = lax.map(one_block, qb)
    o = o.swapaxes(0, 1).reshape(b, l, DA_HEADS, DA_V_DIM)
    o = rms_norm(o, subln_g) * (1.0 - lambda_init)
    return o.reshape(b, l, DA_WIDTH)


def s5_direction(u, a_re, a_im, log_dt, b_re, b_im, c_re, c_im, reverse):
    f32 = jnp.float32
    a_re = a_re.astype(f32)
    a_im = a_im.astype(f32)
    dt = jnp.exp(log_dt.astype(f32))[:, None]
    mag = jnp.exp(a_re * dt)
    ang = a_im * dt
    ab_re = mag * jnp.cos(ang)
    ab_im = mag * jnp.sin(ang)
    den = a_re * a_re + a_im * a_im
    nr = ab_re - 1.0
    coef_re = (nr * a_re + ab_im * a_im) / den
    coef_im = (ab_im * a_re - nr * a_im) / den
    b_re = b_re.astype(f32)
    b_im = b_im.astype(f32)
    bb_re = coef_re[..., None] * b_re - coef_im[..., None] * b_im
    bb_im = coef_re[..., None] * b_im + coef_im[..., None] * b_re
    bu_re = jnp.einsum('lbgh,gph->lbgp', u, bb_re)
    bu_im = jnp.einsum('lbgh,gph->lbgp', u, bb_im)
    l = u.shape[0]
    sa_re = jnp.broadcast_to(ab_re[None, None], (l, 1) + ab_re.shape)
    sa_im = jnp.broadcast_to(ab_im[None, None], (l, 1) + ab_im.shape)

    def combine(e1, e2):
        a1r, a1i, b1r, b1i = e1
        a2r, a2i, b2r, b2i = e2
        return (a2r * a1r - a2i * a1i,
                a2r * a1i + a2i * a1r,
                a2r * b1r - a2i * b1i + b2r,
                a2r * b1i + a2i * b1r + b2i)

    _, _, xr, xi = lax.associative_scan(combine, (sa_re, sa_im, bu_re, bu_im), reverse=reverse, axis=0)
    return (jnp.einsum('lbgp,ghp->lbgh', xr, c_re.astype(f32))
            - jnp.einsum('lbgp,ghp->lbgh', xi, c_im.astype(f32)))


def s5_mixer(u, a_re, a_im, log_dt, b_re, b_im, c_re, c_im, d_skip, w_glu):
    b, l, _ = u.shape
    uf = u.astype(jnp.float32)
    ug = uf.reshape(b, l, SSM_GROUPS, SSM_GROUP).transpose(1, 0, 2, 3)
    y = (s5_direction(ug, a_re[0], a_im[0], log_dt[0], b_re[0], b_im[0], c_re[0], c_im[0], False)
         + s5_direction(ug, a_re[1], a_im[1], log_dt[1], b_re[1], b_im[1], c_re[1], c_im[1], True))
    y = y.transpose(1, 0, 2, 3).reshape(b, l, SSM_WIDTH) + d_skip.astype(jnp.float32) * uf
    y = jax.nn.gelu(y).astype(u.dtype)
    val, gate = jnp.split(y @ w_glu, 2, axis=-1)
    return val * jax.nn.sigmoid(gate)


def hybrid_mixer(hn, w_in, lambda_q1, lambda_k1, lambda_q2, lambda_k2, subln_g,
                 a_re, a_im, log_dt, b_re, b_im, c_re, c_im, d_skip, w_glu,
                 w_branch_attn, w_branch_ssm, w_mix_out, lambda_init):
    b, l, _ = hn.shape
    q, k, v, u, g_a, g_s = jnp.split(hn @ w_in, IN_SPLITS, axis=-1)
    pos = jnp.arange(l, dtype=jnp.int32)
    q = rope(q.reshape(b, l, DA_HEADS, 2, DA_HEAD_DIM), pos)
    k = rope(k.reshape(b, l, DA_HEADS, 2, DA_HEAD_DIM), pos)
    v = v.reshape(b, l, DA_HEADS, DA_V_DIM)
    f32 = jnp.float32
    lam = (jnp.exp(jnp.sum(lambda_q1.astype(f32) * lambda_k1.astype(f32)))
           - jnp.exp(jnp.sum(lambda_q2.astype(f32) * lambda_k2.astype(f32))) + lambda_init)
    o_a = diff_attention(q, k, v, lam, subln_g, lambda_init)
    o_s = s5_mixer(u, a_re, a_im, log_dt, b_re, b_im, c_re, c_im, d_skip, w_glu)
    merged = jax.nn.sigmoid(g_a) * (o_a @ w_branch_attn) + jax.nn.sigmoid(g_s) * (o_s @ w_branch_ssm)
    return merged @ w_mix_out


def cross_attention(hn, mem_n, w_xq, w_xkv, w_xo):
    b, l, _ = hn.shape
    q = (hn @ w_xq).reshape(b, l, X_HEADS, X_HEAD_DIM)
    k, v = jnp.split(mem_n @ w_xkv, 2, axis=-1)
    k = k.reshape(b, -1, X_HEADS, X_HEAD_DIM)
    v = v.reshape(b, -1, X_HEADS, X_HEAD_DIM)
    s = jnp.einsum('bqhd,bkhd->bhqk', q, k, preferred_element_type=jnp.float32) * (X_HEAD_DIM ** -0.5)
    p = jax.nn.softmax(s, axis=-1)
    o = jnp.einsum('bhqk,bkhd->bqhd', p.astype(v.dtype), v).reshape(b, l, D_MODEL)
    return o @ w_xo


def moe_ffn(hn, w_router, b_router, w_e1, b_e1, w_e2, b_e2):
    f32 = jnp.float32
    b, l, d = hn.shape
    t = hn.reshape(-1, d)
    n_tok = t.shape[0]
    n_asg = n_tok * TOP_K
    logits = jnp.dot(t, w_router, preferred_element_type=f32) + b_router.astype(f32)
    top_v, top_i = lax.top_k(logits, TOP_K)
    gates = jax.nn.softmax(top_v, axis=-1)
    flat_e = top_i.reshape(-1).astype(jnp.int32)
    flat_tok = jnp.arange(n_asg, dtype=jnp.int32) // TOP_K
    flat_w = gates.reshape(-1)
    order = jnp.argsort(flat_e)
    sorted_e = flat_e[order]
    counts = jnp.bincount(flat_e, length=N_EXPERTS).astype(jnp.int32)
    padded = (counts + MOE_BLOCK - 1) // MOE_BLOCK * MOE_BLOCK
    pad_end = jnp.cumsum(padded)
    pad_start = pad_end - padded
    grp_start = jnp.cumsum(counts) - counts
    rank = jnp.arange(n_asg, dtype=jnp.int32) - grp_start[sorted_e]
    dest = pad_start[sorted_e] + rank
    n_pad = n_asg + N_EXPERTS * MOE_BLOCK
    n_blk = n_pad // MOE_BLOCK
    row_tok = jnp.zeros((n_pad,), jnp.int32).at[dest].set(flat_tok[order])
    row_w = jnp.zeros((n_pad,), f32).at[dest].set(flat_w[order])
    blk_start = jnp.arange(n_blk, dtype=jnp.int32) * MOE_BLOCK
    blk_e = jnp.minimum(jnp.searchsorted(pad_end, blk_start, side='right'), N_EXPERTS - 1).astype(jnp.int32)
    xs = t[row_tok].reshape(n_blk, MOE_BLOCK, d)

    def expert_block(args):
        xb, e = args
        hid = xb @ w_e1[e] + b_e1[e]
        gate = jnp.minimum(hid[:, :D_FF], SWIGLU_LIMIT)
        lin = jnp.clip(hid[:, D_FF:], -SWIGLU_LIMIT, SWIGLU_LIMIT)
        act = gate * jax.nn.sigmoid(SWIGLU_ALPHA * gate) * (lin + 1.0)
        return act @ w_e2[e] + b_e2[e]

    ys = lax.map(expert_block, (xs, blk_e)).reshape(n_pad, d)
    out = jax.ops.segment_sum(ys.astype(f32) * row_w[:, None], row_tok, num_segments=n_tok)
    return out.astype(hn.dtype).reshape(b, l, d)


def setup_inputs(seed: int = 0) -> dict:
    key = jax.random.key(seed)
    ks = iter(jax.random.split(key, 40))
    f32 = jnp.float32
    L = DEPTH
    G, P, HG = SSM_GROUPS, SSM_STATE, SSM_GROUP

    def nrm(shape, fan_in):
        return jax.random.normal(next(ks), shape, f32) * (fan_in ** -0.5)

    def gain(shape):
        return 1.0 + 0.02 * jax.random.normal(next(ks), shape, f32)

    def small(shape, s):
        return s * jax.random.normal(next(ks), shape, f32)

    n_idx = jnp.arange(SSM_STATE, dtype=f32)
    return {
        'x': jax.random.normal(next(ks), (BATCH, SEQ, D_MODEL), f32),
        'mem': jax.random.normal(next(ks), (BATCH, N_MEM, D_MODEL), f32),
        'norm_mix_g': gain((L, D_MODEL)),
        'w_in': nrm((L, D_MODEL, IN_WIDTH), D_MODEL),
        'lambda_q1': small((L, DA_HEAD_DIM), 0.1),
        'lambda_k1': small((L, DA_HEAD_DIM), 0.1),
        'lambda_q2': small((L, DA_HEAD_DIM), 0.1),
        'lambda_k2': small((L, DA_HEAD_DIM), 0.1),
        'subln_g': gain((L, DA_V_DIM)),
        'ssm_a_re': -0.5 + small((L, 2, G, P), 0.01),
        'ssm_a_im': math.pi * n_idx + small((L, 2, G, P), 0.01),
        'ssm_log_dt': jax.random.uniform(next(ks), (L, 2, G), f32, math.log(1e-3), math.log(1e-1)),
        'ssm_b_re': nrm((L, 2, G, P, HG), 2 * HG),
        'ssm_b_im': nrm((L, 2, G, P, HG), 2 * HG),
        'ssm_c_re': nrm((L, 2, G, HG, P), 2 * P),
        'ssm_c_im': nrm((L, 2, G, HG, P), 2 * P),
        'ssm_d': jax.random.normal(next(ks), (L, SSM_WIDTH), f32),
        'w_glu': nrm((L, SSM_WIDTH, 2 * SSM_WIDTH), SSM_WIDTH),
        'w_branch_attn': nrm((L, DA_WIDTH, D_MODEL), DA_WIDTH),
        'w_branch_ssm': nrm((L, SSM_WIDTH, D_MODEL), SSM_WIDTH),
        'w_mix_out': nrm((L, D_MODEL, D_MODEL), D_MODEL),
        'norm_cross_g': gain((L, D_MODEL)),
        'norm_mem_g': gain((L, D_MODEL)),
        'w_xq': nrm((L, D_MODEL, D_MODEL), D_MODEL),
        'w_xkv': nrm((L, D_MODEL, 2 * D_MODEL), D_MODEL),
        'w_xo': nrm((L, D_MODEL, D_MODEL), D_MODEL),
        'norm_ffn_g': gain((L, D_MODEL)),
        'w_router': nrm((L, D_MODEL, N_EXPERTS), D_MODEL),
        'b_router': small((L, N_EXPERTS), 0.01),
        'w_e1': nrm((L, N_EXPERTS, D_MODEL, 2 * D_FF), D_MODEL),
        'b_e1': small((L, N_EXPERTS, 2 * D_FF), 0.01),
        'w_e2': nrm((L, N_EXPERTS, D_FF, D_MODEL), D_FF),
        'b_e2': small((L, N_EXPERTS, D_MODEL), 0.01),
        'norm_final_g': gain((D_MODEL,)),
    }


def reference(x, mem, norm_mix_g, w_in, lambda_q1, lambda_k1, lambda_q2, lambda_k2, subln_g,
              ssm_a_re, ssm_a_im, ssm_log_dt, ssm_b_re, ssm_b_im, ssm_c_re, ssm_c_im, ssm_d, w_glu,
              w_branch_attn, w_branch_ssm, w_mix_out, norm_cross_g, norm_mem_g, w_xq, w_xkv, w_xo,
              norm_ffn_g, w_router, b_router, w_e1, b_e1, w_e2, b_e2, norm_final_g):
    h = x
    for layer in range(DEPTH):
        lambda_init = 0.8 - 0.6 * math.exp(-0.3 * layer)
        hn = rms_norm(h, norm_mix_g[layer])
        h = h + hybrid_mixer(hn, w_in[layer], lambda_q1[layer], lambda_k1[layer], lambda_q2[layer],
                             lambda_k2[layer], subln_g[layer], ssm_a_re[layer], ssm_a_im[layer],
                             ssm_log_dt[layer], ssm_b_re[layer], ssm_b_im[layer], ssm_c_re[layer],
                             ssm_c_im[layer], ssm_d[layer], w_glu[layer], w_branch_attn[layer],
                             w_branch_ssm[layer], w_mix_out[layer], lambda_init)
        h = h + cross_attention(rms_norm(h, norm_cross_g[layer]), rms_norm(mem, norm_mem_g[layer]),
                                w_xq[layer], w_xkv[layer], w_xo[layer])
        h = h + moe_ffn(rms_norm(h, norm_ffn_g[layer]), w_router[layer], b_router[layer],
                        w_e1[layer], b_e1[layer], w_e2[layer], b_e2[layer])
    return rms_norm(h, norm_final_g)
```

```python
import functools
import math

import jax
import jax.numpy as jnp
from jax import lax
from jax.experimental import pallas as pl
from jax.experimental.pallas import tpu as pltpu

X_HEADS = 4
TOP_K = 4
ROPE_THETA = 10000.0
SWIGLU_LIMIT = 7.0
SWIGLU_ALPHA = 1.702
EPS = 1e-6
LOG2E = 1.4426950408889634

LANES = 128
V7X_VMEM_BYTES = 64 * 1024 * 1024

MOE_ROWS = 512

F32 = jnp.float32
BF16 = jnp.bfloat16
U32 = jnp.uint32
I32 = jnp.int32


def _cparams(semantics, vmem_mb):
    return pltpu.CompilerParams(dimension_semantics=semantics, vmem_limit_bytes=vmem_mb * 1024 * 1024)


def _rms(x, g):
    return x * lax.rsqrt(jnp.mean(x * x, axis=-1, keepdims=True) + EPS) * g


def _dot(a, b):
    return jnp.dot(a, b, preferred_element_type=F32)


def _dot_nt(a, b):
    return lax.dot_general(a, b, (((1,), (1,)), ((), ())), preferred_element_type=F32)


def _pack_bf16_pairs(x):
    c = x.shape[1] // 2
    lo = lax.bitcast_convert_type(x[:, :c].astype(BF16).astype(F32), U32)
    hi = lax.bitcast_convert_type(x[:, c:].astype(BF16).astype(F32), U32)
    return (hi & jnp.uint32(0xFFFF0000)) | (lo >> jnp.uint32(16))


def _unpack_bf16_pairs(w):
    lo = lax.bitcast_convert_type(w << jnp.uint32(16), F32)
    hi = lax.bitcast_convert_type(w & jnp.uint32(0xFFFF0000), F32)
    return jnp.concatenate([lo, hi], axis=1)


def _inproj_kernel(x_ref, g_ref, w_ref, cos_ref, sin_ref, q_ref, k_ref, v_ref, u_ref, ga_ref, gs_ref,
                   *, da_width, ssm_width, d_model, head_dim, q_scale):
    x = x_ref[0]
    hb = _rms(x, g_ref[...]).astype(BF16)
    tl = x.shape[0]
    half = head_dim // 2
    lane = lax.broadcasted_iota(I32, (tl, LANES), 1)
    first = (lane & (head_dim - 1)) < half

    def rope(z, scale):
        outs = []
        for c in range(da_width // LANES):
            zc = z[:, c * LANES:(c + 1) * LANES]
            sw = jnp.where(first, pltpu.roll(zc, LANES - half, 1), pltpu.roll(zc, half, 1))
            r = zc * cos_ref[:, c * LANES:(c + 1) * LANES] + sw * sin_ref[:, c * LANES:(c + 1) * LANES]
            outs.append(r * scale if scale != 1.0 else r)
        return jnp.concatenate(outs, axis=1)

    o = 0
    q_ref[0] = rope(_dot(hb, w_ref[:, o:o + da_width]), q_scale).astype(BF16)
    o += da_width
    k_ref[0] = rope(_dot(hb, w_ref[:, o:o + da_width]), 1.0).astype(BF16)
    o += da_width
    v_ref[0] = _dot(hb, w_ref[:, o:o + da_width]).astype(BF16)
    o += da_width
    u_ref[...] = _dot(hb, w_ref[:, o:o + ssm_width])
    o += ssm_width
    ga_ref[0] = _dot(hb, w_ref[:, o:o + d_model]).astype(BF16)
    o += d_model
    gs_ref[0] = _dot(hb, w_ref[:, o:o + d_model]).astype(BF16)


def _inproj(x, g, w_in, cos_t, sin_t, *, da_width, ssm_width, head_dim, tl):
    b, l, d = x.shape
    in_w = w_in.shape[1]
    kern = functools.partial(_inproj_kernel, da_width=da_width, ssm_width=ssm_width, d_model=d,
                             head_dim=head_dim, q_scale=head_dim ** -0.5 * LOG2E)
    tok = lambda width: pl.BlockSpec((1, tl, width), lambda i, j: (i, j, 0))
    return pl.pallas_call(
        kern,
        grid=(b, l // tl),
        in_specs=[tok(d),
                  pl.BlockSpec((1, d), lambda i, j: (0, 0)),
                  pl.BlockSpec((d, in_w), lambda i, j: (0, 0)),
                  pl.BlockSpec((tl, da_width), lambda i, j: (j, 0)),
                  pl.BlockSpec((tl, da_width), lambda i, j: (j, 0))],
        out_specs=[tok(da_width), tok(da_width), tok(da_width),
                   pl.BlockSpec((tl, ssm_width), lambda i, j: (j, i)),
                   tok(d), tok(d)],
        out_shape=[jax.ShapeDtypeStruct((b, l, da_width), BF16)] * 3
        + [jax.ShapeDtypeStruct((l, b * ssm_width), F32)]
        + [jax.ShapeDtypeStruct((b, l, d), BF16)] * 2,
        compiler_params=_cparams(("parallel", "parallel"), 48),
        name="inproj_rope",
    )(x, g, w_in, cos_t, sin_t)


def _diffattn_kernel(lam_ref, q_ref, k_ref, v_ref, g_ref, o_ref, *, tq, head_dim, lambda_init):
    l = q_ref.shape[1]
    k = k_ref[0]
    v = v_ref[0]
    lp = lam_ref[...]
    lam = (jnp.exp(jnp.sum(lp[0:1] * lp[1:2], axis=-1, keepdims=True))
           - jnp.exp(jnp.sum(lp[2:3] * lp[3:4], axis=-1, keepdims=True)) + lambda_init)
    lane = lax.broadcasted_iota(I32, (tq, 2 * head_dim), 1)
    zero = jnp.zeros((), BF16)
    gain = g_ref[...] * (1.0 - lambda_init)

    def body(i, carry):
        r0 = pl.multiple_of(i * tq, tq)
        q = q_ref[0, pl.ds(r0, tq), :]
        qq = jnp.concatenate([jnp.where(lane < head_dim, q, zero), jnp.where(lane >= head_dim, q, zero)], axis=0)
        s = _dot_nt(qq, k)
        p = jnp.exp2(s - jnp.max(s, axis=-1, keepdims=True))
        r = 1.0 / jnp.sum(p, axis=-1, keepdims=True)
        w = p[:tq] * r[:tq] - p[tq:] * (r[tq:] * lam)
        o = _dot(w.astype(BF16), v)
        o_ref[0, pl.ds(r0, tq), :] = _rms(o, gain).astype(o_ref.dtype)
        return carry

    lax.fori_loop(0, l // tq, body, 0)


def _diffattn(lam_p, q, k, v, subln_g, *, heads, head_dim, lambda_init, tq):
    b, l, w = q.shape
    vd = 2 * head_dim
    kern = functools.partial(_diffattn_kernel, tq=tq, head_dim=head_dim, lambda_init=lambda_init)
    blk = pl.BlockSpec((1, l, vd), lambda i, h: (i, 0, h))
    return pl.pallas_call(
        kern,
        grid=(b, heads),
        in_specs=[pl.BlockSpec((4, head_dim), lambda i, h: (0, 0)), blk, blk, blk,
                  pl.BlockSpec((1, vd), lambda i, h: (0, 0))],
        out_specs=blk,
        out_shape=jax.ShapeDtypeStruct((b, l, w), BF16),
        compiler_params=_cparams(("parallel", "parallel"), 48),
        name="diff_attention",
    )(lam_p, q, k, v, subln_g)


def _s5_disc_kernel(are_ref, aim_ref, ldt_ref, bre_ref, bim_ref, abre_ref, abim_ref, bbre_ref, bbim_ref):
    for d in range(are_ref.shape[0]):
        a_re = are_ref[d]
        a_im = aim_ref[d]
        dt = jnp.exp(ldt_ref[d])
        mag = jnp.exp(a_re * dt)
        ang = a_im * dt
        ab_re = mag * jnp.cos(ang)
        ab_im = mag * jnp.sin(ang)
        den = a_re * a_re + a_im * a_im
        nr = ab_re - 1.0
        coef_re = (nr * a_re + ab_im * a_im) / den
        coef_im = (ab_im * a_re - nr * a_im) / den
        abre_ref[d] = ab_re
        abim_ref[d] = ab_im
        b_re = bre_ref[d]
        b_im = bim_ref[d]
        bbre_ref[d] = coef_re[None] * b_re - coef_im[None] * b_im
        bbim_ref[d] = coef_re[None] * b_im + coef_im[None] * b_re


def _s5_discretize(a_re, a_im, log_dt, b_re, b_im):
    two, g, p, hg = b_re.shape
    bt_re = jnp.transpose(b_re, (0, 3, 1, 2))
    bt_im = jnp.transpose(b_im, (0, 3, 1, 2))
    return pl.pallas_call(
        _s5_disc_kernel,
        out_shape=[jax.ShapeDtypeStruct((two, g, p), F32)] * 2 + [jax.ShapeDtypeStruct((two, hg, g, p), F32)] * 2,
        name="s5_discretize",
    )(a_re.astype(F32), a_im.astype(F32), log_dt.astype(F32)[..., None], bt_re.astype(F32), bt_im.astype(F32))


def _block_diag(m, gb):
    g, r, c = m.shape
    mb = m.reshape(g // gb, gb, r, c)
    eye = jnp.eye(gb, dtype=m.dtype)
    return jnp.einsum('jgrc,gh->jgrhc', mb, eye).reshape(g // gb, gb * r, gb * c)


def _s5_scan_chunk(u_ref, bre_ref, bim_ref, are_ref, aim_ref, cre_ref, cim_ref, sre_ref, sim_ref, bufre, bufim,
                   *, reverse):
    tc, nb, w = u_ref.shape
    nbund = w // LANES
    lw = are_ref.shape[-1]
    ys = []
    for j in range(nbund):
        ub = u_ref[:, :, j * LANES:(j + 1) * LANES].reshape(tc * nb, LANES).astype(BF16)
        bufre[...] = _dot(ub, bre_ref[j])
        bufim[...] = _dot(ub, bim_ref[j])
        a_re = jnp.broadcast_to(are_ref[j], (nb, lw))
        a_im = jnp.broadcast_to(aim_ref[j], (nb, lw))

        def step(t, carry):
            xr, xi = carry
            tt = (tc - 1 - t) if reverse else t
            rows = pl.ds(pl.multiple_of(tt * nb, nb), nb)
            nr = a_re * xr - a_im * xi + bufre[rows, :]
            ni = a_re * xi + a_im * xr + bufim[rows, :]
            bufre[rows, :] = nr
            bufim[rows, :] = ni
            return nr, ni

        xr, xi = lax.fori_loop(0, tc, step, (sre_ref[j], sim_ref[j]), unroll=4)
        sre_ref[j] = xr
        sim_ref[j] = xi
        ys.append(_dot(bufre[...].astype(BF16), cre_ref[j]) - _dot(bufim[...].astype(BF16), cim_ref[j]))
    return ys


def _s5_fwd_kernel(u_ref, bre_ref, bim_ref, are_ref, aim_ref, cre_ref, cim_ref, y_ref, sre_ref, sim_ref,
                   bufre, bufim):
    @pl.when(pl.program_id(0) == 0)
    def _():
        sre_ref[...] = jnp.zeros_like(sre_ref)
        sim_ref[...] = jnp.zeros_like(sim_ref)

    tc, nb, _ = u_ref.shape
    ys = _s5_scan_chunk(u_ref, bre_ref, bim_ref, are_ref, aim_ref, cre_ref, cim_ref, sre_ref, sim_ref,
                        bufre, bufim, reverse=False)
    for j, y in enumerate(ys):
        y_ref[:, :, j * LANES:(j + 1) * LANES] = y.reshape(tc, nb, LANES)


def _s5_bwd_kernel(u_ref, yf_ref, bre_ref, bim_ref, are_ref, aim_ref, cre_ref, cim_ref, d_ref, wglu_ref, o_ref,
                   sre_ref, sim_ref, bufre, bufim):
    @pl.when(pl.program_id(0) == 0)
    def _():
        sre_ref[...] = jnp.zeros_like(sre_ref)
        sim_ref[...] = jnp.zeros_like(sim_ref)

    tc, nb, w = u_ref.shape
    ys = _s5_scan_chunk(u_ref, bre_ref, bim_ref, are_ref, aim_ref, cre_ref, cim_ref, sre_ref, sim_ref,
                        bufre, bufim, reverse=True)
    y = jnp.concatenate(ys, axis=1)
    y = y + yf_ref[...].reshape(tc * nb, w) + d_ref[...] * u_ref[...].reshape(tc * nb, w)
    y = jax.nn.gelu(y).astype(BF16)
    vg = _dot(y, wglu_ref[...])
    o = vg[:, :w] * jax.nn.sigmoid(vg[:, w:])
    o_ref[...] = o.reshape(tc, nb, w).astype(o_ref.dtype)


def _s5_mixer(u3, ab_re, ab_im, bb_re, bb_im, c_re, c_im, d_skip, w_glu, *, tc):
    l, nb, w = u3.shape
    two, hg, g, p = bb_re.shape
    gb = LANES // hg
    nbund = g // gb
    lw = gb * p
    nch = l // tc

    def direction_params(d):
        bre = _block_diag(jnp.swapaxes(bb_re[d], 0, 1), gb).astype(BF16)
        bim = _block_diag(jnp.swapaxes(bb_im[d], 0, 1), gb).astype(BF16)
        cre = _block_diag(jnp.swapaxes(c_re[d], 1, 2), gb).astype(BF16)
        cim = _block_diag(jnp.swapaxes(c_im[d], 1, 2), gb).astype(BF16)
        are = ab_re[d].reshape(nbund, 1, lw)
        aim = ab_im[d].reshape(nbund, 1, lw)
        return bre, bim, are, aim, cre, cim

    full = lambda a: pl.BlockSpec(a.shape, lambda i: (0,) * a.ndim)
    scratch = [pltpu.VMEM((nbund, nb, lw), F32), pltpu.VMEM((nbund, nb, lw), F32),
               pltpu.VMEM((tc * nb, lw), F32), pltpu.VMEM((tc * nb, lw), F32)]

    pf = direction_params(0)
    y_f = pl.pallas_call(
        _s5_fwd_kernel,
        grid=(nch,),
        in_specs=[pl.BlockSpec((tc, nb, w), lambda i: (i, 0, 0))] + [full(a) for a in pf],
        out_specs=pl.BlockSpec((tc, nb, w), lambda i: (i, 0, 0)),
        out_shape=jax.ShapeDtypeStruct((l, nb, w), F32),
        scratch_shapes=scratch,
        compiler_params=_cparams(("arbitrary",), 48),
        name="s5_forward_scan",
    )(u3, *pf)

    pb = direction_params(1)
    rev = lambda i: (nch - 1 - i, 0, 0)
    return pl.pallas_call(
        _s5_bwd_kernel,
        grid=(nch,),
        in_specs=[pl.BlockSpec((tc, nb, w), rev), pl.BlockSpec((tc, nb, w), rev)] + [full(a) for a in pb]
        + [pl.BlockSpec((1, w), lambda i: (0, 0)), pl.BlockSpec(w_glu.shape, lambda i: (0, 0))],
        out_specs=pl.BlockSpec((tc, nb, w), rev),
        out_shape=jax.ShapeDtypeStruct((l, nb, w), BF16),
        scratch_shapes=scratch,
        compiler_params=_cparams(("arbitrary",), 48),
        name="s5_backward_scan_glu",
    )(u3, y_f, *pb, d_skip, w_glu)


def _memkv_kernel(m_ref, g_ref, w_ref, k_ref, v_ref):
    d = m_ref.shape[2]
    mn = _rms(m_ref[0], g_ref[...]).astype(BF16)
    k_ref[0] = _dot(mn, w_ref[:, :d]).astype(BF16)
    v_ref[0] = _dot(mn, w_ref[:, d:]).astype(BF16)


def _memkv(mem, g, w_xkv):
    b, n, d = mem.shape
    blk = pl.BlockSpec((1, n, d), lambda i: (i, 0, 0))
    return pl.pallas_call(
        _memkv_kernel,
        grid=(b,),
        in_specs=[blk, pl.BlockSpec((1, d), lambda i: (0, 0)), pl.BlockSpec((d, 2 * d), lambda i: (0, 0))],
        out_specs=[blk, blk],
        out_shape=[jax.ShapeDtypeStruct((b, n, d), BF16)] * 2,
        compiler_params=_cparams(("parallel",), 32),
        name="mem_kv_proj",
    )(mem, g, w_xkv)


def _mix_cross_kernel(x_ref, oa_ref, os_ref, ga_ref, gs_ref, wba_ref, wbs_ref, wmo_ref, gc_ref, wxq_ref,
                      kx_ref, vx_ref, wxo_ref, gf_ref, wrh_ref, wrl_ref, br_ref,
                      h_ref, hp_ref, lg_ref, *, heads):
    d = x_ref.shape[2]
    hd = d // heads
    merged = (jax.nn.sigmoid(ga_ref[0].astype(F32)) * _dot(oa_ref[0], wba_ref[...])
              + jax.nn.sigmoid(gs_ref[0].astype(F32)) * _dot(os_ref[...], wbs_ref[...]))
    h1 = x_ref[0] + _dot(merged.astype(BF16), wmo_ref[...])

    qx = (_dot(_rms(h1, gc_ref[...]).astype(BF16), wxq_ref[...]) * (hd ** -0.5 * LOG2E)).astype(BF16)
    outs = []
    for hh in range(heads):
        sl = slice(hh * hd, (hh + 1) * hd)
        s = _dot_nt(qx[:, sl], kx_ref[0, :, sl])
        p = jnp.exp2(s - jnp.max(s, axis=-1, keepdims=True))
        p = p * (1.0 / jnp.sum(p, axis=-1, keepdims=True))
        outs.append(_dot(p.astype(BF16), vx_ref[0, :, sl]))
    h2 = h1 + _dot(jnp.concatenate(outs, axis=1).astype(BF16), wxo_ref[...])
    h_ref[0] = h2

    hn = _rms(h2, gf_ref[...])
    hp_ref[...] = _pack_bf16_pairs(hn)
    hi = hn.astype(BF16)
    lo = (hn - hi.astype(F32)).astype(BF16)
    lg_ref[...] = (_dot_nt(wrh_ref[...], hi) + _dot_nt(wrh_ref[...], lo) + _dot_nt(wrl_ref[...], hi)) + br_ref[...]


def _mix_cross(x, o_a, o_s2, g_a, g_s, wba, wbs, wmo, gc, wxq, kx, vx, wxo, gf, wr_hi, wr_lo, b_r, *, tl):
    b, l, d = x.shape
    aw = o_a.shape[2]
    sw = o_s2.shape[1] // b
    n_mem = kx.shape[1]
    e = wr_hi.shape[0]
    nt = l // tl
    kern = functools.partial(_mix_cross_kernel, heads=X_HEADS)
    tok = lambda width: pl.BlockSpec((1, tl, width), lambda i, j: (i, j, 0))
    full = lambda a: pl.BlockSpec(a.shape, lambda i, j: (0,) * a.ndim)
    return pl.pallas_call(
        kern,
        grid=(b, nt),
        in_specs=[tok(d), tok(aw), pl.BlockSpec((tl, sw), lambda i, j: (j, i)), tok(d), tok(d),
                  full(wba), full(wbs), full(wmo), full(gc), full(wxq),
                  pl.BlockSpec((1, n_mem, d), lambda i, j: (i, 0, 0)),
                  pl.BlockSpec((1, n_mem, d), lambda i, j: (i, 0, 0)),
                  full(wxo), full(gf), full(wr_hi), full(wr_lo), full(b_r)],
        out_specs=[tok(d),
                   pl.BlockSpec((tl, d // 2), lambda i, j: (i * nt + j, 0)),
                   pl.BlockSpec((e, tl), lambda i, j: (0, i * nt + j))],
        out_shape=[jax.ShapeDtypeStruct((b, l, d), F32),
                   jax.ShapeDtypeStruct((b * l, d // 2), U32),
                   jax.ShapeDtypeStruct((e, b * l), F32)],
        compiler_params=_cparams(("parallel", "parallel"), 56),
        name="mix_cross_router",
    )(x, o_a, o_s2, g_a, g_s, wba, wbs, wmo, gc, wxq, kx, vx, wxo, gf, wr_hi, wr_lo, b_r)


def _routing_kernel(lg_ref, dest_ref, gate_ref, blke_ref, nused_ref, idx_s, rank_s, cnt_s, *, tr, rows, n_blk):
    e, t = lg_ref.shape
    nt = t // tr
    ie = lax.broadcasted_iota(I32, (e, tr), 0).astype(F32)
    tri = (lax.broadcasted_iota(I32, (tr, tr), 0) <= lax.broadcasted_iota(I32, (tr, tr), 1)).astype(BF16)
    cnt_s[...] = jnp.zeros_like(cnt_s)

    def phase1(i, carry):
        cols = pl.ds(pl.multiple_of(i * tr, tr), tr)
        v = lg_ref[:, cols]
        tops, hots = [], []
        for k in range(TOP_K):
            m = jnp.max(v, axis=0, keepdims=True)
            idx = jnp.min(jnp.where(v == m, ie, float(e)), axis=0, keepdims=True)
            hot = ie == idx
            v = jnp.where(hot, -jnp.inf, v)
            tops.append(m)
            hots.append(hot)
            idx_s[k:k + 1, cols] = idx
        ex = [jnp.exp(m - tops[0]) for m in tops]
        den = ex[0] + ex[1] + ex[2] + ex[3]
        for k in range(TOP_K):
            gate_ref[k:k + 1, cols] = ex[k] / den
        hot_all = hots[0] | hots[1] | hots[2] | hots[3]
        hot_f = jnp.where(hot_all, 1.0, 0.0)
        incl = _dot(hot_f.astype(BF16), tri)
        before = cnt_s[:, 0:1] + incl - hot_f
        for k in range(TOP_K):
            rank_s[k:k + 1, cols] = jnp.sum(jnp.where(hots[k], before, 0.0), axis=0, keepdims=True)
        cnt_s[...] = cnt_s[...] + jnp.sum(hot_f, axis=1, keepdims=True)
        return carry

    lax.fori_loop(0, nt, phase1, 0)

    cnt = cnt_s[...]
    nblk_e = jnp.floor((cnt + (rows - 1.0)) * (1.0 / rows))
    row = lax.broadcasted_iota(I32, cnt.shape, 0)
    incl_b = nblk_e
    s = 1
    while s < e:
        incl_b = incl_b + jnp.where(row >= s, pltpu.roll(incl_b, s, 0), 0.0)
        s *= 2
    start_rows = (incl_b - nblk_e) * float(rows)
    start_col = start_rows[:, 0:1]

    def phase2(i, carry):
        cols = pl.ds(pl.multiple_of(i * tr, tr), tr)
        for k in range(TOP_K):
            hot = ie == idx_s[k:k + 1, cols]
            base = jnp.sum(jnp.where(hot, start_col, 0.0), axis=0, keepdims=True)
            dest_ref[k:k + 1, cols] = (base + rank_s[k:k + 1, cols]).astype(I32)
        return carry

    lax.fori_loop(0, nt, phase2, 0)

    nbp = blke_ref.shape[1]
    jb = lax.broadcasted_iota(I32, (e, nbp), 1).astype(F32)
    ends = incl_b[:, 0:1]
    be = jnp.sum(jnp.where(ends <= jb, 1.0, 0.0), axis=0, keepdims=True)
    blke_ref[...] = jnp.minimum(be, e - 1.0).astype(I32)
    nused_ref[...] = jnp.max(incl_b, axis=0, keepdims=True).astype(I32)


def _routing(logits_t, *, rows, n_blk, tr):
    e, t = logits_t.shape
    nbp = -(-n_blk // LANES) * LANES
    kern = functools.partial(_routing_kernel, tr=tr, rows=rows, n_blk=n_blk)
    return pl.pallas_call(
        kern,
        out_shape=[jax.ShapeDtypeStruct((TOP_K, t), I32), jax.ShapeDtypeStruct((TOP_K, t), F32),
                   jax.ShapeDtypeStruct((1, nbp), I32), jax.ShapeDtypeStruct((1, LANES), I32)],
        scratch_shapes=[pltpu.VMEM((TOP_K, t), F32), pltpu.VMEM((TOP_K, t), F32), pltpu.VMEM((e, LANES), F32)],
        compiler_params=pltpu.CompilerParams(vmem_limit_bytes=32 * 1024 * 1024),
        name="moe_routing",
    )(logits_t)


def _dispatch_kernel(dest_ref, hp_ref, xs_in_ref, xs_ref, sem, *, tm):
    del xs_in_ref

    def issue(i, carry):
        for k in range(TOP_K):
            d = dest_ref[0, 0, i * TOP_K + k]
            pltpu.make_async_copy(hp_ref.at[pl.ds(i, 1)], xs_ref.at[pl.ds(d, 1)], sem).start()
        return carry

    lax.fori_loop(0, tm, issue, 0)
    for k in range(TOP_K):
        pltpu.make_async_copy(hp_ref, xs_ref.at[pl.ds(0, tm)], sem).wait()


def _dispatch(dest_tiles, hp, xs_init, *, tm):
    t, c = hp.shape
    kern = functools.partial(_dispatch_kernel, tm=tm)
    return pl.pallas_call(
        kern,
        grid=(t // tm,),
        in_specs=[pl.BlockSpec((1, 1, tm * TOP_K), lambda i: (i, 0, 0), memory_space=pltpu.SMEM),
                  pl.BlockSpec((tm, c), lambda i: (i, 0)),
                  pl.BlockSpec(memory_space=pl.ANY)],
        out_specs=pl.BlockSpec(memory_space=pl.ANY),
        out_shape=jax.ShapeDtypeStruct(xs_init.shape, xs_init.dtype),
        scratch_shapes=[pltpu.SemaphoreType.DMA],
        input_output_aliases={2: 0},
        compiler_params=pltpu.CompilerParams(dimension_semantics=("arbitrary",)),
        name="moe_dispatch",
    )(dest_tiles, hp, xs_init)


def _expert_kernel(blke_ref, nused_ref, xs_ref, w1_ref, b1_ref, w2_ref, b2_ref, ys_ref):
    @pl.when(pl.program_id(0) < nused_ref[0])
    def _():
        f = w2_ref.shape[1]
        x = _unpack_bf16_pairs(xs_ref[...]).astype(BF16)
        hid = _dot(x, w1_ref[0]) + b1_ref[0]
        gate = jnp.minimum(hid[:, :f], SWIGLU_LIMIT)
        lin = jnp.clip(hid[:, f:], -SWIGLU_LIMIT, SWIGLU_LIMIT)
        act = gate * jax.nn.sigmoid(SWIGLU_ALPHA * gate) * (lin + 1.0)
        y = _dot(act.astype(BF16), w2_ref[0]) + b2_ref[0]
        ys_ref[...] = _pack_bf16_pairs(y)

    @pl.when(pl.program_id(0) >= nused_ref[0])
    def _():
        ys_ref[...] = jnp.zeros_like(ys_ref)


def _experts(blk_e, n_used, xs, w1, b1, w2, b2, *, rows):
    n_pad, c = xs.shape
    e, d, f2 = w1.shape
    f = f2 // 2
    n_blk = n_pad // rows
    row_blk = lambda j, be, nu: (jnp.minimum(j, nu[0] - 1), 0)
    wsel = lambda j, be, nu: (be[j], 0, 0)
    grid_spec = pltpu.PrefetchScalarGridSpec(
        num_scalar_prefetch=2,
        grid=(n_blk,),
        in_specs=[pl.BlockSpec((rows, c), row_blk),
                  pl.BlockSpec((1, d, f2), wsel), pl.BlockSpec((1, 1, f2), wsel),
                  pl.BlockSpec((1, f, d), wsel), pl.BlockSpec((1, 1, d), wsel)],
        out_specs=pl.BlockSpec((rows, c), lambda j, be, nu: (j, 0)),
    )
    return pl.pallas_call(
        _expert_kernel,
        grid_spec=grid_spec,
        out_shape=jax.ShapeDtypeStruct((n_pad, c), U32),
        compiler_params=_cparams(("arbitrary",), 56),
        name="moe_experts",
    )(blk_e, n_used, xs, w1, b1, w2, b2)


def _combine_kernel(dest_ref, h_ref, gate_ref, g_ref, ys_ref, o_ref, stage, sem, *, tm, final_norm):
    def issue(i, carry):
        for k in range(TOP_K):
            d = dest_ref[0, 0, i * TOP_K + k]
            pltpu.make_async_copy(ys_ref.at[pl.ds(d, 1)], stage.at[k, pl.ds(i, 1)], sem).start()
        return carry

    lax.fori_loop(0, tm, issue, 0)
    for k in range(TOP_K):
        pltpu.make_async_copy(ys_ref.at[pl.ds(0, tm)], stage.at[k], sem).wait()
    acc = h_ref[...]
    gates = gate_ref[...]
    moe = jnp.zeros_like(acc)
    for k in range(TOP_K):
        moe = moe + _unpack_bf16_pairs(stage[k]) * gates[:, k:k + 1]
    out = acc + moe
    if final_norm:
        out = _rms(out, g_ref[...])
    o_ref[...] = out


def _combine(dest_tiles, h, gates_t, g_final, ys, *, tm, final_norm):
    t, d = h.shape
    c = ys.shape[1]
    kern = functools.partial(_combine_kernel, tm=tm, final_norm=final_norm)
    return pl.pallas_call(
        kern,
        grid=(t // tm,),
        in_specs=[pl.BlockSpec((1, 1, tm * TOP_K), lambda i: (i, 0, 0), memory_space=pltpu.SMEM),
                  pl.BlockSpec((tm, d), lambda i: (i, 0)),
                  pl.BlockSpec((tm, TOP_K), lambda i: (i, 0)),
                  pl.BlockSpec((1, d), lambda i: (0, 0)),
                  pl.BlockSpec(memory_space=pl.ANY)],
        out_specs=pl.BlockSpec((tm, d), lambda i: (i, 0)),
        out_shape=jax.ShapeDtypeStruct((t, d), F32),
        scratch_shapes=[pltpu.VMEM((TOP_K, tm, c), U32), pltpu.SemaphoreType.DMA],
        compiler_params=_cparams(("arbitrary",), 32),
        name="moe_combine",
    )(dest_tiles, h, gates_t, g_final, ys)


def _pick_tile(n, pref):
    t = min(n, pref)
    while n % t:
        t //= 2
    return t


def kernel(x, mem, norm_mix_g, w_in, lambda_q1, lambda_k1, lambda_q2, lambda_k2, subln_g, ssm_a_re, ssm_a_im, ssm_log_dt, ssm_b_re, ssm_b_im, ssm_c_re, ssm_c_im, ssm_d, w_glu, w_branch_attn, w_branch_ssm, w_mix_out, norm_cross_g, norm_mem_g, w_xq, w_xkv, w_xo, norm_ffn_g, w_router, b_router, w_e1, b_e1, w_e2, b_e2, norm_final_g):
    b, l, d = x.shape
    depth = w_in.shape[0]
    head_dim = lambda_q1.shape[-1]
    da_width = w_branch_attn.shape[1]
    heads = da_width // (2 * head_dim)
    ssm_width = w_branch_ssm.shape[1]
    n_exp = w_router.shape[-1]
    t = b * l
    assert d % (2 * LANES) == 0 and da_width % LANES == 0 and ssm_width % LANES == 0 and LANES % head_dim == 0
    assert head_dim & (head_dim - 1) == 0

    tl = _pick_tile(l, 512)
    tq = _pick_tile(l, 256)
    tc = _pick_tile(l, 64)
    tm = _pick_tile(t, 256)
    tr = _pick_tile(t, 512)
    rows = MOE_ROWS
    n_blk = (t * TOP_K) // rows + n_exp
    n_pad = n_blk * rows

    half = head_dim // 2
    inv = ROPE_THETA ** (-jnp.arange(half, dtype=F32) * (2.0 / head_dim))
    ang = jnp.arange(l, dtype=F32)[:, None] * inv[None, :]
    cos_t = jnp.tile(jnp.concatenate([jnp.cos(ang), jnp.cos(ang)], axis=1), (1, da_width // head_dim))
    sin_t = jnp.tile(jnp.concatenate([-jnp.sin(ang), jnp.sin(ang)], axis=1), (1, da_width // head_dim))

    row = lambda v: v.astype(F32).reshape(1, -1)
    h = x.astype(F32)
    out = None
    for layer in range(depth):
        lambda_init = 0.8 - 0.6 * math.exp(-0.3 * layer)
        q, k, v, u2, g_a, g_s = _inproj(h, row(norm_mix_g[layer]), w_in[layer].astype(BF16), cos_t, sin_t,
                                        da_width=da_width, ssm_width=ssm_width, head_dim=head_dim, tl=tl)
        lam_p = jnp.stack([lambda_q1[layer], lambda_k1[layer], lambda_q2[layer], lambda_k2[layer]]).astype(F32)
        o_a = _diffattn(lam_p, q, k, v, row(subln_g[layer]), heads=heads, head_dim=head_dim,
                        lambda_init=lambda_init, tq=tq)

        ab_re, ab_im, bb_re, bb_im = _s5_discretize(ssm_a_re[layer], ssm_a_im[layer], ssm_log_dt[layer],
                                                    ssm_b_re[layer], ssm_b_im[layer])
        o_s = _s5_mixer(u2.reshape(l, b, ssm_width), ab_re, ab_im, bb_re, bb_im,
                        ssm_c_re[layer].astype(F32), ssm_c_im[layer].astype(F32),
                        row(ssm_d[layer]), w_glu[layer].astype(BF16), tc=tc)

        kx, vx = _memkv(mem.astype(F32), row(norm_mem_g[layer]), w_xkv[layer].astype(BF16))
        wr_t = w_router[layer].astype(F32).T
        wr_hi = wr_t.astype(BF16)
        wr_lo = (wr_t - wr_hi.astype(F32)).astype(BF16)
        h2, hp, logits_t = _mix_cross(
            h, o_a, o_s.reshape(l, b * ssm_width), g_a, g_s,
            w_branch_attn[layer].astype(BF16), w_branch_ssm[layer].astype(BF16), w_mix_out[layer].astype(BF16),
            row(norm_cross_g[layer]), w_xq[layer].astype(BF16), kx, vx, w_xo[layer].astype(BF16),
            row(norm_ffn_g[layer]), wr_hi, wr_lo, b_router[layer].astype(F32).reshape(n_exp, 1), tl=tl)

        dest, gates, blk_e, n_used = _routing(logits_t, rows=rows, n_blk=n_blk, tr=tr)
        dest_tiles = dest.T.reshape(t // tm, 1, tm * TOP_K)
        xs = _dispatch(dest_tiles, hp, jnp.zeros((n_pad, d // 2), U32), tm=tm)
        ys = _experts(blk_e[0, :n_blk], n_used[0, :1], xs,
                      w_e1[layer].astype(BF16), b_e1[layer].astype(F32)[:, None, :],
                      w_e2[layer].astype(BF16), b_e2[layer].astype(F32)[:, None, :], rows=rows)
        last = layer == depth - 1
        h_flat = _combine(dest_tiles, h2.reshape(t, d), gates.T, row(norm_final_g), ys, tm=tm, final_norm=last)
        h = h_flat.reshape(b, l, d)
        out = h
    return out.astype(x.dtype)
```

```python
import functools
import math

import jax
import jax.numpy as jnp
from jax import lax
from jax.experimental import pallas as pl
from jax.experimental.pallas import tpu as pltpu

X_HEADS = 4
TOP_K = 4
ROPE_THETA = 10000.0
SWIGLU_LIMIT = 7.0
SWIGLU_ALPHA = 1.702
EPS = 1e-6
LOG2E = 1.4426950408889634

LANES = 128
V7X_VMEM_BYTES = 64 * 1024 * 1024

MOE_ROWS = 512

F32 = jnp.float32
BF16 = jnp.bfloat16
U32 = jnp.uint32
I32 = jnp.int32


def _cparams(semantics, vmem_mb):
    return pltpu.CompilerParams(dimension_semantics=semantics, vmem_limit_bytes=vmem_mb * 1024 * 1024)


def _rms(x, g):
    return x * lax.rsqrt(jnp.mean(x * x, axis=-1, keepdims=True) + EPS) * g


def _dot(a, b):
    return jnp.dot(a, b, preferred_element_type=F32)


def _dot_nt(a, b):
    return lax.dot_general(a, b, (((1,), (1,)), ((), ())), preferred_element_type=F32)


def _pack_bf16_pairs(x):
    c = x.shape[1] // 2
    lo = lax.bitcast_convert_type(x[:, :c].astype(BF16).astype(F32), U32)
    hi = lax.bitcast_convert_type(x[:, c:].astype(BF16).astype(F32), U32)
    return (hi & jnp.uint32(0xFFFF0000)) | (lo >> jnp.uint32(16))


def _unpack_bf16_pairs(w):
    lo = lax.bitcast_convert_type(w << jnp.uint32(16), F32)
    hi = lax.bitcast_convert_type(w & jnp.uint32(0xFFFF0000), F32)
    return jnp.concatenate([lo, hi], axis=1)


def _inproj_kernel(x_ref, g_ref, w_ref, cos_ref, sin_ref, q_ref, k_ref, v_ref, u_ref, ga_ref, gs_ref,
                   *, da_width, ssm_width, d_model, head_dim, q_scale):
    x = x_ref[0]
    hb = _rms(x, g_ref[...]).astype(BF16)
    tl = x.shape[0]
    half = head_dim // 2
    lane = lax.broadcasted_iota(I32, (tl, LANES), 1)
    first = (lane & (head_dim - 1)) < half

    def rope(z, scale):
        outs = []
        for c in range(da_width // LANES):
            zc = z[:, c * LANES:(c + 1) * LANES]
            sw = jnp.where(first, pltpu.roll(zc, LANES - half, 1), pltpu.roll(zc, half, 1))
            r = zc * cos_ref[:, c * LANES:(c + 1) * LANES] + sw * sin_ref[:, c * LANES:(c + 1) * LANES]
            outs.append(r * scale if scale != 1.0 else r)
        return jnp.concatenate(outs, axis=1)

    o = 0
    q_ref[0] = rope(_dot(hb, w_ref[:, o:o + da_width]), q_scale).astype(BF16)
    o += da_width
    k_ref[0] = rope(_dot(hb, w_ref[:, o:o + da_width]), 1.0).astype(BF16)
    o += da_width
    v_ref[0] = _dot(hb, w_ref[:, o:o + da_width]).astype(BF16)
    o += da_width
    u_ref[...] = _dot(hb, w_ref[:, o:o + ssm_width])
    o += ssm_width
    ga_ref[0] = _dot(hb, w_ref[:, o:o + d_model]).astype(BF16)
    o += d_model
    gs_ref[0] = _dot(hb, w_ref[:, o:o + d_model]).astype(BF16)


def _inproj(x, g, w_in, cos_t, sin_t, *, da_width, ssm_width, head_dim, tl):
    b, l, d = x.shape
    in_w = w_in.shape[1]
    kern = functools.partial(_inproj_kernel, da_width=da_width, ssm_width=ssm_width, d_model=d,
                             head_dim=head_dim, q_scale=head_dim ** -0.5 * LOG2E)
    tok = lambda width: pl.BlockSpec((1, tl, width), lambda i, j: (i, j, 0))
    return pl.pallas_call(
        kern,
        grid=(b, l // tl),
        in_specs=[tok(d),
                  pl.BlockSpec((1, d), lambda i, j: (0, 0)),
                  pl.BlockSpec((d, in_w), lambda i, j: (0, 0)),
                  pl.BlockSpec((tl, da_width), lambda i, j: (j, 0)),
                  pl.BlockSpec((tl, da_width), lambda i, j: (j, 0))],
        out_specs=[tok(da_width), tok(da_width), tok(da_width),
                   pl.BlockSpec((tl, ssm_width), lambda i, j: (j, i)),
                   tok(d), tok(d)],
        out_shape=[jax.ShapeDtypeStruct((b, l, da_width), BF16)] * 3
        + [jax.ShapeDtypeStruct((l, b * ssm_width), F32)]
        + [jax.ShapeDtypeStruct((b, l, d), BF16)] * 2,
        compiler_params=_cparams(("parallel", "parallel"), 48),
        name="inproj_rope",
    )(x, g, w_in, cos_t, sin_t)


def _diffattn_kernel(lam_ref, q_ref, k_ref, v_ref, g_ref, o_ref, *, tq, head_dim, lambda_init):
    l = q_ref.shape[1]
    vd = 2 * head_dim
    k = k_ref[0]
    v = v_ref[0]
    lp = lam_ref[...]
    lam = (jnp.exp(jnp.sum(lp[0:1] * lp[1:2], axis=-1, keepdims=True))
           - jnp.exp(jnp.sum(lp[2:3] * lp[3:4], axis=-1, keepdims=True)) + lambda_init)
    lane = lax.broadcasted_iota(I32, (tq, vd), 1)
    zero = jnp.zeros((), BF16)
    gain = g_ref[...] * (1.0 - lambda_init)

    def body(i, carry):
        r0 = pl.multiple_of(i * tq, tq)
        q = q_ref[0, pl.ds(r0, tq), :]
        def component(qm):
            s = _dot_nt(qm, k)
            p = jnp.exp2(s - jnp.max(s, axis=-1, keepdims=True))
            r = 1.0 / jnp.sum(p, axis=-1, keepdims=True)
            return _dot(p.astype(BF16), v) * r

        o = component(jnp.where(lane < head_dim, q, zero)) - lam * component(jnp.where(lane >= head_dim, q, zero))
        o_ref[0, pl.ds(r0, tq), :] = _rms(o, gain).astype(o_ref.dtype)
        return carry

    lax.fori_loop(0, l // tq, body, 0, unroll=4)


def _diffattn(lam_p, q, k, v, subln_g, *, heads, head_dim, lambda_init, tq):
    b, l, w = q.shape
    vd = 2 * head_dim
    kern = functools.partial(_diffattn_kernel, tq=tq, head_dim=head_dim, lambda_init=lambda_init)
    blk = pl.BlockSpec((1, l, vd), lambda i, h: (i, 0, h))
    return pl.pallas_call(
        kern,
        grid=(b, heads),
        in_specs=[pl.BlockSpec((4, head_dim), lambda i, h: (0, 0)), blk, blk, blk,
                  pl.BlockSpec((1, vd), lambda i, h: (0, 0))],
        out_specs=blk,
        out_shape=jax.ShapeDtypeStruct((b, l, w), BF16),
        compiler_params=_cparams(("parallel", "parallel"), 48),
        name="diff_attention",
    )(lam_p, q, k, v, subln_g)


def _s5_disc_kernel(are_ref, aim_ref, ldt_ref, bre_ref, bim_ref, abre_ref, abim_ref, bbre_ref, bbim_ref):
    for d in range(are_ref.shape[0]):
        a_re = are_ref[d]
        a_im = aim_ref[d]
        dt = jnp.exp(ldt_ref[d])
        mag = jnp.exp(a_re * dt)
        ang = a_im * dt
        ab_re = mag * jnp.cos(ang)
        ab_im = mag * jnp.sin(ang)
        den = a_re * a_re + a_im * a_im
        nr = ab_re - 1.0
        coef_re = (nr * a_re + ab_im * a_im) / den
        coef_im = (ab_im * a_re - nr * a_im) / den
        abre_ref[d] = ab_re
        abim_ref[d] = ab_im
        b_re = bre_ref[d]
        b_im = bim_ref[d]
        bbre_ref[d] = coef_re[None] * b_re - coef_im[None] * b_im
        bbim_ref[d] = coef_re[None] * b_im + coef_im[None] * b_re


def _s5_discretize(a_re, a_im, log_dt, b_re, b_im):
    two, g, p, hg = b_re.shape
    bt_re = jnp.transpose(b_re, (0, 3, 1, 2))
    bt_im = jnp.transpose(b_im, (0, 3, 1, 2))
    return pl.pallas_call(
        _s5_disc_kernel,
        out_shape=[jax.ShapeDtypeStruct((two, g, p), F32)] * 2 + [jax.ShapeDtypeStruct((two, hg, g, p), F32)] * 2,
        name="s5_discretize",
    )(a_re.astype(F32), a_im.astype(F32), log_dt.astype(F32)[..., None], bt_re.astype(F32), bt_im.astype(F32))


def _block_diag(m, gb):
    g, r, c = m.shape
    mb = m.reshape(g // gb, gb, r, c)
    eye = jnp.eye(gb, dtype=m.dtype)
    return jnp.einsum('jgrc,gh->jgrhc', mb, eye).reshape(g // gb, gb * r, gb * c)


def _s5_scan_chunk(u_ref, bre_ref, bim_ref, are_ref, aim_ref, cre_ref, cim_ref, sre_ref, sim_ref, bufre, bufim,
                   *, reverse):
    tc, nb, w = u_ref.shape
    nbund = w // LANES
    lw = are_ref.shape[-1]
    ys = []
    for j in range(nbund):
        ub = u_ref[:, :, j * LANES:(j + 1) * LANES].reshape(tc * nb, LANES).astype(BF16)
        bufre[...] = _dot(ub, bre_ref[j])
        bufim[...] = _dot(ub, bim_ref[j])
        a_re = jnp.broadcast_to(are_ref[j], (nb, lw))
        a_im = jnp.broadcast_to(aim_ref[j], (nb, lw))

        def step(t, carry):
            xr, xi = carry
            tt = (tc - 1 - t) if reverse else t
            rows = pl.ds(pl.multiple_of(tt * nb, nb), nb)
            nr = a_re * xr - a_im * xi + bufre[rows, :]
            ni = a_re * xi + a_im * xr + bufim[rows, :]
            bufre[rows, :] = nr
            bufim[rows, :] = ni
            return nr, ni

        xr, xi = lax.fori_loop(0, tc, step, (sre_ref[j], sim_ref[j]), unroll=4)
        sre_ref[j] = xr
        sim_ref[j] = xi
        ys.append(_dot(bufre[...].astype(BF16), cre_ref[j]) - _dot(bufim[...].astype(BF16), cim_ref[j]))
    return ys


def _s5_fwd_kernel(u_ref, bre_ref, bim_ref, are_ref, aim_ref, cre_ref, cim_ref, y_ref, sre_ref, sim_ref,
                   bufre, bufim):
    @pl.when(pl.program_id(0) == 0)
    def _():
        sre_ref[...] = jnp.zeros_like(sre_ref)
        sim_ref[...] = jnp.zeros_like(sim_ref)

    tc, nb, _ = u_ref.shape
    ys = _s5_scan_chunk(u_ref, bre_ref, bim_ref, are_ref, aim_ref, cre_ref, cim_ref, sre_ref, sim_ref,
                        bufre, bufim, reverse=False)
    for j, y in enumerate(ys):
        y_ref[:, :, j * LANES:(j + 1) * LANES] = y.reshape(tc, nb, LANES)


def _s5_bwd_kernel(u_ref, yf_ref, bre_ref, bim_ref, are_ref, aim_ref, cre_ref, cim_ref, d_ref, wglu_ref, o_ref,
                   sre_ref, sim_ref, bufre, bufim):
    @pl.when(pl.program_id(0) == 0)
    def _():
        sre_ref[...] = jnp.zeros_like(sre_ref)
        sim_ref[...] = jnp.zeros_like(sim_ref)

    tc, nb, w = u_ref.shape
    ys = _s5_scan_chunk(u_ref, bre_ref, bim_ref, are_ref, aim_ref, cre_ref, cim_ref, sre_ref, sim_ref,
                        bufre, bufim, reverse=True)
    y = jnp.concatenate(ys, axis=1)
    y = y + yf_ref[...].reshape(tc * nb, w) + d_ref[...] * u_ref[...].reshape(tc * nb, w)
    y = jax.nn.gelu(y).astype(BF16)
    vg = _dot(y, wglu_ref[...])
    o = vg[:, :w] * jax.nn.sigmoid(vg[:, w:])
    o_ref[...] = o.reshape(tc, nb, w).astype(o_ref.dtype)


def _s5_mixer(u3, ab_re, ab_im, bb_re, bb_im, c_re, c_im, d_skip, w_glu, *, tc):
    l, nb, w = u3.shape
    two, hg, g, p = bb_re.shape
    gb = LANES // hg
    nbund = g // gb
    lw = gb * p
    nch = l // tc

    def direction_params(d):
        bre = _block_diag(jnp.swapaxes(bb_re[d], 0, 1), gb).astype(BF16)
        bim = _block_diag(jnp.swapaxes(bb_im[d], 0, 1), gb).astype(BF16)
        cre = _block_diag(jnp.swapaxes(c_re[d], 1, 2), gb).astype(BF16)
        cim = _block_diag(jnp.swapaxes(c_im[d], 1, 2), gb).astype(BF16)
        are = ab_re[d].reshape(nbund, 1, lw)
        aim = ab_im[d].reshape(nbund, 1, lw)
        return bre, bim, are, aim, cre, cim

    full = lambda a: pl.BlockSpec(a.shape, lambda i: (0,) * a.ndim)
    scratch = [pltpu.VMEM((nbund, nb, lw), F32), pltpu.VMEM((nbund, nb, lw), F32),
               pltpu.VMEM((tc * nb, lw), F32), pltpu.VMEM((tc * nb, lw), F32)]

    pf = direction_params(0)
    y_f = pl.pallas_call(
        _s5_fwd_kernel,
        grid=(nch,),
        in_specs=[pl.BlockSpec((tc, nb, w), lambda i: (i, 0, 0))] + [full(a) for a in pf],
        out_specs=pl.BlockSpec((tc, nb, w), lambda i: (i, 0, 0)),
        out_shape=jax.ShapeDtypeStruct((l, nb, w), F32),
        scratch_shapes=scratch,
        compiler_params=_cparams(("arbitrary",), 48),
        name="s5_forward_scan",
    )(u3, *pf)

    pb = direction_params(1)
    rev = lambda i: (nch - 1 - i, 0, 0)
    return pl.pallas_call(
        _s5_bwd_kernel,
        grid=(nch,),
        in_specs=[pl.BlockSpec((tc, nb, w), rev), pl.BlockSpec((tc, nb, w), rev)] + [full(a) for a in pb]
        + [pl.BlockSpec((1, w), lambda i: (0, 0)), pl.BlockSpec(w_glu.shape, lambda i: (0, 0))],
        out_specs=pl.BlockSpec((tc, nb, w), rev),
        out_shape=jax.ShapeDtypeStruct((l, nb, w), BF16),
        scratch_shapes=scratch,
        compiler_params=_cparams(("arbitrary",), 48),
        name="s5_backward_scan_glu",
    )(u3, y_f, *pb, d_skip, w_glu)


def _memkv_kernel(m_ref, g_ref, w_ref, k_ref, v_ref):
    d = m_ref.shape[2]
    mn = _rms(m_ref[0], g_ref[...]).astype(BF16)
    k_ref[0] = _dot(mn, w_ref[:, :d]).astype(BF16)
    v_ref[0] = _dot(mn, w_ref[:, d:]).astype(BF16)


def _memkv(mem, g, w_xkv):
    b, n, d = mem.shape
    blk = pl.BlockSpec((1, n, d), lambda i: (i, 0, 0))
    return pl.pallas_call(
        _memkv_kernel,
        grid=(b,),
        in_specs=[blk, pl.BlockSpec((1, d), lambda i: (0, 0)), pl.BlockSpec((d, 2 * d), lambda i: (0, 0))],
        out_specs=[blk, blk],
        out_shape=[jax.ShapeDtypeStruct((b, n, d), BF16)] * 2,
        compiler_params=_cparams(("parallel",), 32),
        name="mem_kv_proj",
    )(mem, g, w_xkv)


def _mix_cross_kernel(x_ref, oa_ref, os_ref, ga_ref, gs_ref, wba_ref, wbs_ref, wmo_ref, gc_ref, wxq_ref,
                      kx_ref, vx_ref, wxo_ref, gf_ref, wrh_ref, wrl_ref, br_ref,
                      h_ref, hp_ref, lg_ref, *, heads):
    d = x_ref.shape[2]
    hd = d // heads
    merged = (jax.nn.sigmoid(ga_ref[0].astype(F32)) * _dot(oa_ref[0], wba_ref[...])
              + jax.nn.sigmoid(gs_ref[0].astype(F32)) * _dot(os_ref[...], wbs_ref[...]))
    h1 = x_ref[0] + _dot(merged.astype(BF16), wmo_ref[...])

    qx = (_dot(_rms(h1, gc_ref[...]).astype(BF16), wxq_ref[...]) * (hd ** -0.5 * LOG2E)).astype(BF16)
    outs = []
    for hh in range(heads):
        sl = slice(hh * hd, (hh + 1) * hd)
        s = _dot_nt(qx[:, sl], kx_ref[0, :, sl])
        p = jnp.exp2(s - jnp.max(s, axis=-1, keepdims=True))
        p = p * (1.0 / jnp.sum(p, axis=-1, keepdims=True))
        outs.append(_dot(p.astype(BF16), vx_ref[0, :, sl]))
    h2 = h1 + _dot(jnp.concatenate(outs, axis=1).astype(BF16), wxo_ref[...])
    h_ref[0] = h2

    hn = _rms(h2, gf_ref[...])
    hp_ref[...] = _pack_bf16_pairs(hn)
    hi = hn.astype(BF16)
    lo = (hn - hi.astype(F32)).astype(BF16)
    lg_ref[...] = (_dot_nt(wrh_ref[...], hi) + _dot_nt(wrh_ref[...], lo) + _dot_nt(wrl_ref[...], hi)) + br_ref[...]


def _mix_cross(x, o_a, o_s2, g_a, g_s, wba, wbs, wmo, gc, wxq, kx, vx, wxo, gf, wr_hi, wr_lo, b_r, *, tl):
    b, l, d = x.shape
    aw = o_a.shape[2]
    sw = o_s2.shape[1] // b
    n_mem = kx.shape[1]
    e = wr_hi.shape[0]
    nt = l // tl
    kern = functools.partial(_mix_cross_kernel, heads=X_HEADS)
    tok = lambda width: pl.BlockSpec((1, tl, width), lambda i, j: (i, j, 0))
    full = lambda a: pl.BlockSpec(a.shape, lambda i, j: (0,) * a.ndim)
    return pl.pallas_call(
        kern,
        grid=(b, nt),
        in_specs=[tok(d), tok(aw), pl.BlockSpec((tl, sw), lambda i, j: (j, i)), tok(d), tok(d),
                  full(wba), full(wbs), full(wmo), full(gc), full(wxq),
                  pl.BlockSpec((1, n_mem, d), lambda i, j: (i, 0, 0)),
                  pl.BlockSpec((1, n_mem, d), lambda i, j: (i, 0, 0)),
                  full(wxo), full(gf), full(wr_hi), full(wr_lo), full(b_r)],
        out_specs=[tok(d),
                   pl.BlockSpec((tl, d // 2), lambda i, j: (i * nt + j, 0)),
                   pl.BlockSpec((e, tl), lambda i, j: (0, i * nt + j))],
        out_shape=[jax.ShapeDtypeStruct((b, l, d), F32),
                   jax.ShapeDtypeStruct((b * l, d // 2), U32),
                   jax.ShapeDtypeStruct((e, b * l), F32)],
        compiler_params=_cparams(("parallel", "parallel"), 56),
        name="mix_cross_router",
    )(x, o_a, o_s2, g_a, g_s, wba, wbs, wmo, gc, wxq, kx, vx, wxo, gf, wr_hi, wr_lo, b_r)


def _routing_kernel(lg_ref, dest_ref, gate_ref, blke_ref, nused_ref, idx_s, rank_s, cnt_s, *, tr, rows, n_blk):
    e, t = lg_ref.shape
    nt = t // tr
    ie = lax.broadcasted_iota(I32, (e, tr), 0).astype(F32)
    tri = (lax.broadcasted_iota(I32, (tr, tr), 0) <= lax.broadcasted_iota(I32, (tr, tr), 1)).astype(BF16)
    cnt_s[...] = jnp.zeros_like(cnt_s)

    def phase1(i, carry):
        cols = pl.ds(pl.multiple_of(i * tr, tr), tr)
        v = lg_ref[:, cols]
        tops, hots = [], []
        for k in range(TOP_K):
            m = jnp.max(v, axis=0, keepdims=True)
            idx = jnp.min(jnp.where(v == m, ie, float(e)), axis=0, keepdims=True)
            hot = ie == idx
            v = jnp.where(hot, -jnp.inf, v)
            tops.append(m)
            hots.append(hot)
            idx_s[k:k + 1, cols] = idx
        ex = [jnp.exp(m - tops[0]) for m in tops]
        den = ex[0] + ex[1] + ex[2] + ex[3]
        for k in range(TOP_K):
            gate_ref[k:k + 1, cols] = ex[k] / den
        hot_all = hots[0] | hots[1] | hots[2] | hots[3]
        hot_f = jnp.where(hot_all, 1.0, 0.0)
        incl = _dot(hot_f.astype(BF16), tri)
        before = cnt_s[:, 0:1] + incl - hot_f
        for k in range(TOP_K):
            rank_s[k:k + 1, cols] = jnp.sum(jnp.where(hots[k], before, 0.0), axis=0, keepdims=True)
        cnt_s[...] = cnt_s[...] + jnp.sum(hot_f, axis=1, keepdims=True)
        return carry

    lax.fori_loop(0, nt, phase1, 0)

    cnt = cnt_s[...]
    nblk_e = jnp.floor((cnt + (rows - 1.0)) * (1.0 / rows))
    row = lax.broadcasted_iota(I32, cnt.shape, 0)
    incl_b = nblk_e
    s = 1
    while s < e:
        incl_b = incl_b + jnp.where(row >= s, pltpu.roll(incl_b, s, 0), 0.0)
        s *= 2
    start_rows = (incl_b - nblk_e) * float(rows)
    start_col = start_rows[:, 0:1]

    def phase2(i, carry):
        cols = pl.ds(pl.multiple_of(i * tr, tr), tr)
        for k in range(TOP_K):
            hot = ie == idx_s[k:k + 1, cols]
            base = jnp.sum(jnp.where(hot, start_col, 0.0), axis=0, keepdims=True)
            dest_ref[k:k + 1, cols] = (base + rank_s[k:k + 1, cols]).astype(I32)
        return carry

    lax.fori_loop(0, nt, phase2, 0)

    nbp = blke_ref.shape[1]
    jb = lax.broadcasted_iota(I32, (e, nbp), 1).astype(F32)
    ends = incl_b[:, 0:1]
    be = jnp.sum(jnp.where(ends <= jb, 1.0, 0.0), axis=0, keepdims=True)
    blke_ref[...] = jnp.minimum(be, e - 1.0).astype(I32)
    nused_ref[...] = jnp.max(incl_b, axis=0, keepdims=True).astype(I32)


def _routing(logits_t, *, rows, n_blk, tr):
    e, t = logits_t.shape
    nbp = -(-n_blk // LANES) * LANES
    kern = functools.partial(_routing_kernel, tr=tr, rows=rows, n_blk=n_blk)
    return pl.pallas_call(
        kern,
        out_shape=[jax.ShapeDtypeStruct((TOP_K, t), I32), jax.ShapeDtypeStruct((TOP_K, t), F32),
                   jax.ShapeDtypeStruct((1, nbp), I32), jax.ShapeDtypeStruct((1, LANES), I32)],
        scratch_shapes=[pltpu.VMEM((TOP_K, t), F32), pltpu.VMEM((TOP_K, t), F32), pltpu.VMEM((e, LANES), F32)],
        compiler_params=pltpu.CompilerParams(vmem_limit_bytes=32 * 1024 * 1024),
        name="moe_routing",
    )(logits_t)


def _dispatch_kernel(dest_ref, hp_ref, xs_in_ref, xs_ref, sem, *, tm):
    del xs_in_ref

    def issue(i, carry):
        for k in range(TOP_K):
            d = dest_ref[0, 0, i * TOP_K + k]
            pltpu.make_async_copy(hp_ref.at[pl.ds(i, 1)], xs_ref.at[pl.ds(d, 1)], sem).start()
        return carry

    lax.fori_loop(0, tm, issue, 0)
    for k in range(TOP_K):
        pltpu.make_async_copy(hp_ref, xs_ref.at[pl.ds(0, tm)], sem).wait()


def _dispatch(dest_tiles, hp, xs_init, *, tm):
    t, c = hp.shape
    kern = functools.partial(_dispatch_kernel, tm=tm)
    return pl.pallas_call(
        kern,
        grid=(t // tm,),
        in_specs=[pl.BlockSpec((1, 1, tm * TOP_K), lambda i: (i, 0, 0), memory_space=pltpu.SMEM),
                  pl.BlockSpec((tm, c), lambda i: (i, 0)),
                  pl.BlockSpec(memory_space=pl.ANY)],
        out_specs=pl.BlockSpec(memory_space=pl.ANY),
        out_shape=jax.ShapeDtypeStruct(xs_init.shape, xs_init.dtype),
        scratch_shapes=[pltpu.SemaphoreType.DMA],
        input_output_aliases={2: 0},
        compiler_params=pltpu.CompilerParams(dimension_semantics=("arbitrary",)),
        name="moe_dispatch",
    )(dest_tiles, hp, xs_init)


def _expert_kernel(blke_ref, nused_ref, xs_ref, w1_ref, b1_ref, w2_ref, b2_ref, ys_ref, w1b_ref, w2b_ref):
    j = pl.program_id(0)

    @pl.when((j == 0) | (blke_ref[j] != blke_ref[jnp.maximum(j - 1, 0)]))
    def _():
        w1b_ref[...] = w1_ref[0].astype(BF16)
        w2b_ref[...] = w2_ref[0].astype(BF16)

    @pl.when(j < nused_ref[0])
    def _():
        f = w2_ref.shape[1]
        x = _unpack_bf16_pairs(xs_ref[...]).astype(BF16)
        hid = _dot(x, w1b_ref[...]) + b1_ref[0]
        gate = jnp.minimum(hid[:, :f], SWIGLU_LIMIT)
        lin = jnp.clip(hid[:, f:], -SWIGLU_LIMIT, SWIGLU_LIMIT)
        act = gate * jax.nn.sigmoid(SWIGLU_ALPHA * gate) * (lin + 1.0)
        y = _dot(act.astype(BF16), w2b_ref[...]) + b2_ref[0]
        ys_ref[...] = _pack_bf16_pairs(y)

    @pl.when(pl.program_id(0) >= nused_ref[0])
    def _():
        ys_ref[...] = jnp.zeros_like(ys_ref)


def _experts(blk_e, n_used, xs, w1, b1, w2, b2, *, rows):
    n_pad, c = xs.shape
    e, d, f2 = w1.shape
    f = f2 // 2
    n_blk = n_pad // rows
    row_blk = lambda j, be, nu: (jnp.minimum(j, nu[0] - 1), 0)
    wsel = lambda j, be, nu: (be[j], 0, 0)
    grid_spec = pltpu.PrefetchScalarGridSpec(
        num_scalar_prefetch=2,
        grid=(n_blk,),
        in_specs=[pl.BlockSpec((rows, c), row_blk),
                  pl.BlockSpec((1, d, f2), wsel), pl.BlockSpec((1, 1, f2), wsel),
                  pl.BlockSpec((1, f, d), wsel), pl.BlockSpec((1, 1, d), wsel)],
        out_specs=pl.BlockSpec((rows, c), lambda j, be, nu: (j, 0)),
        scratch_shapes=[pltpu.VMEM((d, f2), BF16), pltpu.VMEM((f, d), BF16)],
    )
    return pl.pallas_call(
        _expert_kernel,
        grid_spec=grid_spec,
        out_shape=jax.ShapeDtypeStruct((n_pad, c), U32),
        compiler_params=_cparams(("arbitrary",), 60),
        name="moe_experts",
    )(blk_e, n_used, xs, w1, b1, w2, b2)


def _combine_kernel(dest_ref, h_ref, gate_ref, g_ref, ys_ref, o_ref, stage, sem, *, tm, final_norm):
    def issue(i, carry):
        for k in range(TOP_K):
            d = dest_ref[0, 0, i * TOP_K + k]
            pltpu.make_async_copy(ys_ref.at[pl.ds(d, 1)], stage.at[k, pl.ds(i, 1)], sem).start()
        return carry

    lax.fori_loop(0, tm, issue, 0)
    for k in range(TOP_K):
        pltpu.make_async_copy(ys_ref.at[pl.ds(0, tm)], stage.at[k], sem).wait()
    acc = h_ref[...]
    gates = gate_ref[...]
    moe = jnp.zeros_like(acc)
    for k in range(TOP_K):
        moe = moe + _unpack_bf16_pairs(stage[k]) * gates[:, k:k + 1]
    out = acc + moe
    if final_norm:
        out = _rms(out, g_ref[...])
    o_ref[...] = out


def _combine(dest_tiles, h, gates_t, g_final, ys, *, tm, final_norm):
    t, d = h.shape
    c = ys.shape[1]
    kern = functools.partial(_combine_kernel, tm=tm, final_norm=final_norm)
    return pl.pallas_call(
        kern,
        grid=(t // tm,),
        in_specs=[pl.BlockSpec((1, 1, tm * TOP_K), lambda i: (i, 0, 0), memory_space=pltpu.SMEM),
                  pl.BlockSpec((tm, d), lambda i: (i, 0)),
                  pl.BlockSpec((tm, TOP_K), lambda i: (i, 0)),
                  pl.BlockSpec((1, d), lambda i: (0, 0)),
                  pl.BlockSpec(memory_space=pl.ANY)],
        out_specs=pl.BlockSpec((tm, d), lambda i: (i, 0)),
        out_shape=jax.ShapeDtypeStruct((t, d), F32),
        scratch_shapes=[pltpu.VMEM((TOP_K, tm, c), U32), pltpu.SemaphoreType.DMA],
        compiler_params=_cparams(("arbitrary",), 32),
        name="moe_combine",
    )(dest_tiles, h, gates_t, g_final, ys)


def _pick_tile(n, pref):
    t = min(n, pref)
    while n % t:
        t //= 2
    return t


def kernel(x, mem, norm_mix_g, w_in, lambda_q1, lambda_k1, lambda_q2, lambda_k2, subln_g, ssm_a_re, ssm_a_im, ssm_log_dt, ssm_b_re, ssm_b_im, ssm_c_re, ssm_c_im, ssm_d, w_glu, w_branch_attn, w_branch_ssm, w_mix_out, norm_cross_g, norm_mem_g, w_xq, w_xkv, w_xo, norm_ffn_g, w_router, b_router, w_e1, b_e1, w_e2, b_e2, norm_final_g):
    b, l, d = x.shape
    depth = w_in.shape[0]
    head_dim = lambda_q1.shape[-1]
    da_width = w_branch_attn.shape[1]
    heads = da_width // (2 * head_dim)
    ssm_width = w_branch_ssm.shape[1]
    n_exp = w_router.shape[-1]
    t = b * l
    assert d % (2 * LANES) == 0 and da_width % LANES == 0 and ssm_width % LANES == 0 and LANES % head_dim == 0
    assert head_dim & (head_dim - 1) == 0

    tl = _pick_tile(l, 512)
    tq = _pick_tile(l, 256)
    tc = _pick_tile(l, 64)
    tm = _pick_tile(t, 256)
    tr = _pick_tile(t, 512)
    rows = MOE_ROWS
    n_blk = (t * TOP_K) // rows + n_exp
    n_pad = n_blk * rows

    half = head_dim // 2
    inv = ROPE_THETA ** (-jnp.arange(half, dtype=F32) * (2.0 / head_dim))
    ang = jnp.arange(l, dtype=F32)[:, None] * inv[None, :]
    cos_t = jnp.tile(jnp.concatenate([jnp.cos(ang), jnp.cos(ang)], axis=1), (1, da_width // head_dim))
    sin_t = jnp.tile(jnp.concatenate([-jnp.sin(ang), jnp.sin(ang)], axis=1), (1, da_width // head_dim))

    row = lambda v: v.astype(F32).reshape(1, -1)
    h = x.astype(F32)
    out = None
    for layer in range(depth):
        lambda_init = 0.8 - 0.6 * math.exp(-0.3 * layer)
        q, k, v, u2, g_a, g_s = _inproj(h, row(norm_mix_g[layer]), w_in[layer].astype(BF16), cos_t, sin_t,
                                        da_width=da_width, ssm_width=ssm_width, head_dim=head_dim, tl=tl)
        lam_p = jnp.stack([lambda_q1[layer], lambda_k1[layer], lambda_q2[layer], lambda_k2[layer]]).astype(F32)
        o_a = _diffattn(lam_p, q, k, v, row(subln_g[layer]), heads=heads, head_dim=head_dim,
                        lambda_init=lambda_init, tq=tq)

        ab_re, ab_im, bb_re, bb_im = _s5_discretize(ssm_a_re[layer], ssm_a_im[layer], ssm_log_dt[layer],
                                                    ssm_b_re[layer], ssm_b_im[layer])
        o_s = _s5_mixer(u2.reshape(l, b, ssm_width), ab_re, ab_im, bb_re, bb_im,
                        ssm_c_re[layer].astype(F32), ssm_c_im[layer].astype(F32),
                        row(ssm_d[layer]), w_glu[layer].astype(BF16), tc=tc)

        kx, vx = _memkv(mem.astype(F32), row(norm_mem_g[layer]), w_xkv[layer].astype(BF16))
        wr_t = w_router[layer].astype(F32).T
        wr_hi = wr_t.astype(BF16)
        wr_lo = (wr_t - wr_hi.astype(F32)).astype(BF16)
        h2, hp, logits_t = _mix_cross(
            h, o_a, o_s.reshape(l, b * ssm_width), g_a, g_s,
            w_branch_attn[layer].astype(BF16), w_branch_ssm[layer].astype(BF16), w_mix_out[layer].astype(BF16),
            row(norm_cross_g[layer]), w_xq[layer].astype(BF16), kx, vx, w_xo[layer].astype(BF16),
            row(norm_ffn_g[layer]), wr_hi, wr_lo, b_router[layer].astype(F32).reshape(n_exp, 1), tl=tl)

        dest, gates, blk_e, n_used = _routing(logits_t, rows=rows, n_blk=n_blk, tr=tr)
        dest_tiles = dest.T.reshape(t // tm, 1, tm * TOP_K)
        xs = _dispatch(dest_tiles, hp, jnp.zeros((n_pad, d // 2), U32), tm=tm)
        ys = _experts(blk_e[0, :n_blk], n_used[0, :1], xs,
                      w_e1[layer].astype(F32), b_e1[layer].astype(F32)[:, None, :],
                      w_e2[layer].astype(F32), b_e2[layer].astype(F32)[:, None, :], rows=rows)
        last = layer == depth - 1
        h_flat = _combine(dest_tiles, h2.reshape(t, d), gates.T, row(norm_final_g), ys, tm=tm, final_norm=last)
        h = h_flat.reshape(b, l, d)
        out = h
    return out.astype(x.dtype)
```

```python
import functools
import math

import jax
import jax.numpy as jnp
from jax import lax
from jax.experimental import pallas as pl
from jax.experimental.pallas import tpu as pltpu
from jax.experimental.pallas import tpu_sc as plsc

X_HEADS = 4
TOP_K = 4
ROPE_THETA = 10000.0
SWIGLU_LIMIT = 7.0
SWIGLU_ALPHA = 1.702
EPS = 1e-6
LOG2E = 1.4426950408889634

LANES = 128
V7X_VMEM_BYTES = 64 * 1024 * 1024
V7X_SC_CORES = 2
V7X_SC_SUBCORES = 16
SC_CHUNK = 128

MOE_ROWS = 512

F32 = jnp.float32
BF16 = jnp.bfloat16
U32 = jnp.uint32
I32 = jnp.int32


def _cparams(semantics, vmem_mb):
    return pltpu.CompilerParams(dimension_semantics=semantics, vmem_limit_bytes=vmem_mb * 1024 * 1024)


def _rms(x, g):
    return x * lax.rsqrt(jnp.mean(x * x, axis=-1, keepdims=True) + EPS) * g


def _dot(a, b):
    return jnp.dot(a, b, preferred_element_type=F32)


def _dot_nt(a, b):
    return lax.dot_general(a, b, (((1,), (1,)), ((), ())), preferred_element_type=F32)


def _pack_bf16_pairs(x):
    c = x.shape[1] // 2
    lo = lax.bitcast_convert_type(x[:, :c].astype(BF16).astype(F32), U32)
    hi = lax.bitcast_convert_type(x[:, c:].astype(BF16).astype(F32), U32)
    return (hi & jnp.uint32(0xFFFF0000)) | (lo >> jnp.uint32(16))


def _unpack_bf16_pairs(w):
    lo = lax.bitcast_convert_type(w << jnp.uint32(16), F32)
    hi = lax.bitcast_convert_type(w & jnp.uint32(0xFFFF0000), F32)
    return jnp.concatenate([lo, hi], axis=1)


def _inproj_kernel(x_ref, g_ref, w_ref, cos_ref, sin_ref, q_ref, k_ref, v_ref, u_ref, ga_ref, gs_ref,
                   *, da_width, ssm_width, d_model, head_dim, q_scale):
    x = x_ref[0]
    hb = _rms(x, g_ref[...]).astype(BF16)
    tl = x.shape[0]
    half = head_dim // 2
    lane = lax.broadcasted_iota(I32, (tl, LANES), 1)
    first = (lane & (head_dim - 1)) < half

    def rope(z, scale):
        outs = []
        for c in range(da_width // LANES):
            zc = z[:, c * LANES:(c + 1) * LANES]
            sw = jnp.where(first, pltpu.roll(zc, LANES - half, 1), pltpu.roll(zc, half, 1))
            r = zc * cos_ref[:, c * LANES:(c + 1) * LANES] + sw * sin_ref[:, c * LANES:(c + 1) * LANES]
            outs.append(r * scale if scale != 1.0 else r)
        return jnp.concatenate(outs, axis=1)

    o = 0
    q_ref[0] = rope(_dot(hb, w_ref[:, o:o + da_width]), q_scale).astype(BF16)
    o += da_width
    k_ref[0] = rope(_dot(hb, w_ref[:, o:o + da_width]), 1.0).astype(BF16)
    o += da_width
    v_ref[0] = _dot(hb, w_ref[:, o:o + da_width]).astype(BF16)
    o += da_width
    u_ref[...] = _dot(hb, w_ref[:, o:o + ssm_width])
    o += ssm_width
    ga_ref[0] = _dot(hb, w_ref[:, o:o + d_model]).astype(BF16)
    o += d_model
    gs_ref[0] = _dot(hb, w_ref[:, o:o + d_model]).astype(BF16)


def _inproj(x, g, w_in, cos_t, sin_t, *, da_width, ssm_width, head_dim, tl):
    b, l, d = x.shape
    in_w = w_in.shape[1]
    kern = functools.partial(_inproj_kernel, da_width=da_width, ssm_width=ssm_width, d_model=d,
                             head_dim=head_dim, q_scale=head_dim ** -0.5 * LOG2E)
    tok = lambda width: pl.BlockSpec((1, tl, width), lambda i, j: (i, j, 0))
    return pl.pallas_call(
        kern,
        grid=(b, l // tl),
        in_specs=[tok(d),
                  pl.BlockSpec((1, d), lambda i, j: (0, 0)),
                  pl.BlockSpec((d, in_w), lambda i, j: (0, 0)),
                  pl.BlockSpec((tl, da_width), lambda i, j: (j, 0)),
                  pl.BlockSpec((tl, da_width), lambda i, j: (j, 0))],
        out_specs=[tok(da_width), tok(da_width), tok(da_width),
                   pl.BlockSpec((tl, ssm_width), lambda i, j: (j, i)),
                   tok(d), tok(d)],
        out_shape=[jax.ShapeDtypeStruct((b, l, da_width), BF16)] * 3
        + [jax.ShapeDtypeStruct((l, b * ssm_width), F32)]
        + [jax.ShapeDtypeStruct((b, l, d), BF16)] * 2,
        compiler_params=_cparams(("parallel", "parallel"), 48),
        name="inproj_rope",
    )(x, g, w_in, cos_t, sin_t)


def _diffattn_kernel(lam_ref, q_ref, k_ref, v_ref, g_ref, o_ref, *, tq, head_dim, lambda_init):
    l = q_ref.shape[1]
    vd = 2 * head_dim
    k = k_ref[0]
    v = v_ref[0]
    lp = lam_ref[...]
    lam = (jnp.exp(jnp.sum(lp[0:1] * lp[1:2], axis=-1, keepdims=True))
           - jnp.exp(jnp.sum(lp[2:3] * lp[3:4], axis=-1, keepdims=True)) + lambda_init)
    lane = lax.broadcasted_iota(I32, (tq, vd), 1)
    zero = jnp.zeros((), BF16)
    gain = g_ref[...] * (1.0 - lambda_init)

    def body(i, carry):
        r0 = pl.multiple_of(i * tq, tq)
        q = q_ref[0, pl.ds(r0, tq), :]
        def component(qm):
            s = _dot_nt(qm, k)
            p = jnp.exp2(s - jnp.max(s, axis=-1, keepdims=True))
            r = 1.0 / jnp.sum(p, axis=-1, keepdims=True)
            return _dot(p.astype(BF16), v) * r

        o = component(jnp.where(lane < head_dim, q, zero)) - lam * component(jnp.where(lane >= head_dim, q, zero))
        o_ref[0, pl.ds(r0, tq), :] = _rms(o, gain).astype(o_ref.dtype)
        return carry

    lax.fori_loop(0, l // tq, body, 0, unroll=4)


def _diffattn(lam_p, q, k, v, subln_g, *, heads, head_dim, lambda_init, tq):
    b, l, w = q.shape
    vd = 2 * head_dim
    kern = functools.partial(_diffattn_kernel, tq=tq, head_dim=head_dim, lambda_init=lambda_init)
    blk = pl.BlockSpec((1, l, vd), lambda i, h: (i, 0, h))
    return pl.pallas_call(
        kern,
        grid=(b, heads),
        in_specs=[pl.BlockSpec((4, head_dim), lambda i, h: (0, 0)), blk, blk, blk,
                  pl.BlockSpec((1, vd), lambda i, h: (0, 0))],
        out_specs=blk,
        out_shape=jax.ShapeDtypeStruct((b, l, w), BF16),
        compiler_params=_cparams(("parallel", "parallel"), 48),
        name="diff_attention",
    )(lam_p, q, k, v, subln_g)


def _s5_disc_kernel(are_ref, aim_ref, ldt_ref, bre_ref, bim_ref, abre_ref, abim_ref, bbre_ref, bbim_ref):
    for d in range(are_ref.shape[0]):
        a_re = are_ref[d]
        a_im = aim_ref[d]
        dt = jnp.exp(ldt_ref[d])
        mag = jnp.exp(a_re * dt)
        ang = a_im * dt
        ab_re = mag * jnp.cos(ang)
        ab_im = mag * jnp.sin(ang)
        den = a_re * a_re + a_im * a_im
        nr = ab_re - 1.0
        coef_re = (nr * a_re + ab_im * a_im) / den
        coef_im = (ab_im * a_re - nr * a_im) / den
        abre_ref[d] = ab_re
        abim_ref[d] = ab_im
        b_re = bre_ref[d]
        b_im = bim_ref[d]
        bbre_ref[d] = coef_re[None] * b_re - coef_im[None] * b_im
        bbim_ref[d] = coef_re[None] * b_im + coef_im[None] * b_re


def _s5_discretize(a_re, a_im, log_dt, b_re, b_im):
    two, g, p, hg = b_re.shape
    bt_re = jnp.transpose(b_re, (0, 3, 1, 2))
    bt_im = jnp.transpose(b_im, (0, 3, 1, 2))
    return pl.pallas_call(
        _s5_disc_kernel,
        out_shape=[jax.ShapeDtypeStruct((two, g, p), F32)] * 2 + [jax.ShapeDtypeStruct((two, hg, g, p), F32)] * 2,
        name="s5_discretize",
    )(a_re.astype(F32), a_im.astype(F32), log_dt.astype(F32)[..., None], bt_re.astype(F32), bt_im.astype(F32))


def _block_diag(m, gb):
    g, r, c = m.shape
    mb = m.reshape(g // gb, gb, r, c)
    eye = jnp.eye(gb, dtype=m.dtype)
    return jnp.einsum('jgrc,gh->jgrhc', mb, eye).reshape(g // gb, gb * r, gb * c)


def _s5_scan_chunk(u_ref, bre_ref, bim_ref, are_ref, aim_ref, cre_ref, cim_ref, sre_ref, sim_ref, bufre, bufim,
                   *, reverse):
    tc, nb, w = u_ref.shape
    nbund = w // LANES
    lw = are_ref.shape[-1]
    ys = []
    for j in range(nbund):
        ub = u_ref[:, :, j * LANES:(j + 1) * LANES].reshape(tc * nb, LANES).astype(BF16)
        bufre[...] = _dot(ub, bre_ref[j])
        bufim[...] = _dot(ub, bim_ref[j])
        a_re = jnp.broadcast_to(are_ref[j], (nb, lw))
        a_im = jnp.broadcast_to(aim_ref[j], (nb, lw))

        def step(t, carry):
            xr, xi = carry
            tt = (tc - 1 - t) if reverse else t
            rows = pl.ds(pl.multiple_of(tt * nb, nb), nb)
            nr = a_re * xr - a_im * xi + bufre[rows, :]
            ni = a_re * xi + a_im * xr + bufim[rows, :]
            bufre[rows, :] = nr
            bufim[rows, :] = ni
            return nr, ni

        xr, xi = lax.fori_loop(0, tc, step, (sre_ref[j], sim_ref[j]), unroll=4)
        sre_ref[j] = xr
        sim_ref[j] = xi
        ys.append(_dot(bufre[...].astype(BF16), cre_ref[j]) - _dot(bufim[...].astype(BF16), cim_ref[j]))
    return ys


def _s5_fwd_kernel(u_ref, bre_ref, bim_ref, are_ref, aim_ref, cre_ref, cim_ref, y_ref, sre_ref, sim_ref,
                   bufre, bufim):
    @pl.when(pl.program_id(0) == 0)
    def _():
        sre_ref[...] = jnp.zeros_like(sre_ref)
        sim_ref[...] = jnp.zeros_like(sim_ref)

    tc, nb, _ = u_ref.shape
    ys = _s5_scan_chunk(u_ref, bre_ref, bim_ref, are_ref, aim_ref, cre_ref, cim_ref, sre_ref, sim_ref,
                        bufre, bufim, reverse=False)
    for j, y in enumerate(ys):
        y_ref[:, :, j * LANES:(j + 1) * LANES] = y.reshape(tc, nb, LANES)


def _s5_bwd_kernel(u_ref, yf_ref, bre_ref, bim_ref, are_ref, aim_ref, cre_ref, cim_ref, d_ref, wglu_ref, o_ref,
                   sre_ref, sim_ref, bufre, bufim):
    @pl.when(pl.program_id(0) == 0)
    def _():
        sre_ref[...] = jnp.zeros_like(sre_ref)
        sim_ref[...] = jnp.zeros_like(sim_ref)

    tc, nb, w = u_ref.shape
    ys = _s5_scan_chunk(u_ref, bre_ref, bim_ref, are_ref, aim_ref, cre_ref, cim_ref, sre_ref, sim_ref,
                        bufre, bufim, reverse=True)
    y = jnp.concatenate(ys, axis=1)
    y = y + yf_ref[...].reshape(tc * nb, w) + d_ref[...] * u_ref[...].reshape(tc * nb, w)
    y = jax.nn.gelu(y).astype(BF16)
    vg = _dot(y, wglu_ref[...])
    o = vg[:, :w] * jax.nn.sigmoid(vg[:, w:])
    o_ref[...] = o.reshape(tc, nb, w).astype(o_ref.dtype)


def _s5_mixer(u3, ab_re, ab_im, bb_re, bb_im, c_re, c_im, d_skip, w_glu, *, tc):
    l, nb, w = u3.shape
    two, hg, g, p = bb_re.shape
    gb = LANES // hg
    nbund = g // gb
    lw = gb * p
    nch = l // tc

    def direction_params(d):
        bre = _block_diag(jnp.swapaxes(bb_re[d], 0, 1), gb).astype(BF16)
        bim = _block_diag(jnp.swapaxes(bb_im[d], 0, 1), gb).astype(BF16)
        cre = _block_diag(jnp.swapaxes(c_re[d], 1, 2), gb).astype(BF16)
        cim = _block_diag(jnp.swapaxes(c_im[d], 1, 2), gb).astype(BF16)
        are = ab_re[d].reshape(nbund, 1, lw)
        aim = ab_im[d].reshape(nbund, 1, lw)
        return bre, bim, are, aim, cre, cim

    full = lambda a: pl.BlockSpec(a.shape, lambda i: (0,) * a.ndim)
    scratch = [pltpu.VMEM((nbund, nb, lw), F32), pltpu.VMEM((nbund, nb, lw), F32),
               pltpu.VMEM((tc * nb, lw), F32), pltpu.VMEM((tc * nb, lw), F32)]

    pf = direction_params(0)
    y_f = pl.pallas_call(
        _s5_fwd_kernel,
        grid=(nch,),
        in_specs=[pl.BlockSpec((tc, nb, w), lambda i: (i, 0, 0))] + [full(a) for a in pf],
        out_specs=pl.BlockSpec((tc, nb, w), lambda i: (i, 0, 0)),
        out_shape=jax.ShapeDtypeStruct((l, nb, w), F32),
        scratch_shapes=scratch,
        compiler_params=_cparams(("arbitrary",), 48),
        name="s5_forward_scan",
    )(u3, *pf)

    pb = direction_params(1)
    rev = lambda i: (nch - 1 - i, 0, 0)
    return pl.pallas_call(
        _s5_bwd_kernel,
        grid=(nch,),
        in_specs=[pl.BlockSpec((tc, nb, w), rev), pl.BlockSpec((tc, nb, w), rev)] + [full(a) for a in pb]
        + [pl.BlockSpec((1, w), lambda i: (0, 0)), pl.BlockSpec(w_glu.shape, lambda i: (0, 0))],
        out_specs=pl.BlockSpec((tc, nb, w), rev),
        out_shape=jax.ShapeDtypeStruct((l, nb, w), BF16),
        scratch_shapes=scratch,
        compiler_params=_cparams(("arbitrary",), 48),
        name="s5_backward_scan_glu",
    )(u3, y_f, *pb, d_skip, w_glu)


def _memkv_kernel(m_ref, g_ref, w_ref, k_ref, v_ref):
    d = m_ref.shape[2]
    mn = _rms(m_ref[0], g_ref[...]).astype(BF16)
    k_ref[0] = _dot(mn, w_ref[:, :d]).astype(BF16)
    v_ref[0] = _dot(mn, w_ref[:, d:]).astype(BF16)


def _memkv(mem, g, w_xkv):
    b, n, d = mem.shape
    blk = pl.BlockSpec((1, n, d), lambda i: (i, 0, 0))
    return pl.pallas_call(
        _memkv_kernel,
        grid=(b,),
        in_specs=[blk, pl.BlockSpec((1, d), lambda i: (0, 0)), pl.BlockSpec((d, 2 * d), lambda i: (0, 0))],
        out_specs=[blk, blk],
        out_shape=[jax.ShapeDtypeStruct((b, n, d), BF16)] * 2,
        compiler_params=_cparams(("parallel",), 32),
        name="mem_kv_proj",
    )(mem, g, w_xkv)


def _mix_cross_kernel(x_ref, oa_ref, os_ref, ga_ref, gs_ref, wba_ref, wbs_ref, wmo_ref, gc_ref, wxq_ref,
                      kx_ref, vx_ref, wxo_ref, gf_ref, wrh_ref, wrl_ref, br_ref,
                      h_ref, hp_ref, lg_ref, *, heads):
    d = x_ref.shape[2]
    hd = d // heads
    merged = (jax.nn.sigmoid(ga_ref[0].astype(F32)) * _dot(oa_ref[0], wba_ref[...])
              + jax.nn.sigmoid(gs_ref[0].astype(F32)) * _dot(os_ref[...], wbs_ref[...]))
    h1 = x_ref[0] + _dot(merged.astype(BF16), wmo_ref[...])

    qx = (_dot(_rms(h1, gc_ref[...]).astype(BF16), wxq_ref[...]) * (hd ** -0.5 * LOG2E)).astype(BF16)
    outs = []
    for hh in range(heads):
        sl = slice(hh * hd, (hh + 1) * hd)
        s = _dot_nt(qx[:, sl], kx_ref[0, :, sl])
        p = jnp.exp2(s - jnp.max(s, axis=-1, keepdims=True))
        p = p * (1.0 / jnp.sum(p, axis=-1, keepdims=True))
        outs.append(_dot(p.astype(BF16), vx_ref[0, :, sl]))
    h2 = h1 + _dot(jnp.concatenate(outs, axis=1).astype(BF16), wxo_ref[...])
    h_ref[0] = h2

    hn = _rms(h2, gf_ref[...])
    hp_ref[...] = _pack_bf16_pairs(hn)
    hi = hn.astype(BF16)
    lo = (hn - hi.astype(F32)).astype(BF16)
    lg_ref[...] = (_dot_nt(wrh_ref[...], hi) + _dot_nt(wrh_ref[...], lo) + _dot_nt(wrl_ref[...], hi)) + br_ref[...]


def _mix_cross(x, o_a, o_s2, g_a, g_s, wba, wbs, wmo, gc, wxq, kx, vx, wxo, gf, wr_hi, wr_lo, b_r, *, tl):
    b, l, d = x.shape
    aw = o_a.shape[2]
    sw = o_s2.shape[1] // b
    n_mem = kx.shape[1]
    e = wr_hi.shape[0]
    nt = l // tl
    kern = functools.partial(_mix_cross_kernel, heads=X_HEADS)
    tok = lambda width: pl.BlockSpec((1, tl, width), lambda i, j: (i, j, 0))
    full = lambda a: pl.BlockSpec(a.shape, lambda i, j: (0,) * a.ndim)
    return pl.pallas_call(
        kern,
        grid=(b, nt),
        in_specs=[tok(d), tok(aw), pl.BlockSpec((tl, sw), lambda i, j: (j, i)), tok(d), tok(d),
                  full(wba), full(wbs), full(wmo), full(gc), full(wxq),
                  pl.BlockSpec((1, n_mem, d), lambda i, j: (i, 0, 0)),
                  pl.BlockSpec((1, n_mem, d), lambda i, j: (i, 0, 0)),
                  full(wxo), full(gf), full(wr_hi), full(wr_lo), full(b_r)],
        out_specs=[tok(d),
                   pl.BlockSpec((tl, d // 2), lambda i, j: (i * nt + j, 0)),
                   pl.BlockSpec((e, tl), lambda i, j: (0, i * nt + j))],
        out_shape=[jax.ShapeDtypeStruct((b, l, d), F32),
                   jax.ShapeDtypeStruct((b * l, d // 2), U32),
                   jax.ShapeDtypeStruct((e, b * l), F32)],
        compiler_params=_cparams(("parallel", "parallel"), 56),
        name="mix_cross_router",
    )(x, o_a, o_s2, g_a, g_s, wba, wbs, wmo, gc, wxq, kx, vx, wxo, gf, wr_hi, wr_lo, b_r)


def _routing_kernel(lg_ref, dest_ref, gate_ref, blke_ref, nused_ref, idx_s, rank_s, cnt_s, *, tr, rows, n_blk):
    e, t = lg_ref.shape
    nt = t // tr
    ie = lax.broadcasted_iota(I32, (e, tr), 0).astype(F32)
    tri = (lax.broadcasted_iota(I32, (tr, tr), 0) <= lax.broadcasted_iota(I32, (tr, tr), 1)).astype(BF16)
    cnt_s[...] = jnp.zeros_like(cnt_s)

    def phase1(i, carry):
        cols = pl.ds(pl.multiple_of(i * tr, tr), tr)
        v = lg_ref[:, cols]
        tops, hots = [], []
        for k in range(TOP_K):
            m = jnp.max(v, axis=0, keepdims=True)
            idx = jnp.min(jnp.where(v == m, ie, float(e)), axis=0, keepdims=True)
            hot = ie == idx
            v = jnp.where(hot, -jnp.inf, v)
            tops.append(m)
            hots.append(hot)
            idx_s[k:k + 1, cols] = idx
        ex = [jnp.exp(m - tops[0]) for m in tops]
        den = ex[0] + ex[1] + ex[2] + ex[3]
        for k in range(TOP_K):
            gate_ref[k:k + 1, cols] = ex[k] / den
        hot_all = hots[0] | hots[1] | hots[2] | hots[3]
        hot_f = jnp.where(hot_all, 1.0, 0.0)
        incl = _dot(hot_f.astype(BF16), tri)
        before = cnt_s[:, 0:1] + incl - hot_f
        for k in range(TOP_K):
            rank_s[k:k + 1, cols] = jnp.sum(jnp.where(hots[k], before, 0.0), axis=0, keepdims=True)
        cnt_s[...] = cnt_s[...] + jnp.sum(hot_f, axis=1, keepdims=True)
        return carry

    lax.fori_loop(0, nt, phase1, 0)

    cnt = cnt_s[...]
    nblk_e = jnp.floor((cnt + (rows - 1.0)) * (1.0 / rows))
    row = lax.broadcasted_iota(I32, cnt.shape, 0)
    incl_b = nblk_e
    s = 1
    while s < e:
        incl_b = incl_b + jnp.where(row >= s, pltpu.roll(incl_b, s, 0), 0.0)
        s *= 2
    start_rows = (incl_b - nblk_e) * float(rows)
    start_col = start_rows[:, 0:1]

    def phase2(i, carry):
        cols = pl.ds(pl.multiple_of(i * tr, tr), tr)
        for k in range(TOP_K):
            hot = ie == idx_s[k:k + 1, cols]
            base = jnp.sum(jnp.where(hot, start_col, 0.0), axis=0, keepdims=True)
            dest_ref[k:k + 1, cols] = (base + rank_s[k:k + 1, cols]).astype(I32)
        return carry

    lax.fori_loop(0, nt, phase2, 0)

    nbp = blke_ref.shape[1]
    jb = lax.broadcasted_iota(I32, (e, nbp), 1).astype(F32)
    ends = incl_b[:, 0:1]
    be = jnp.sum(jnp.where(ends <= jb, 1.0, 0.0), axis=0, keepdims=True)
    blke_ref[...] = jnp.minimum(be, e - 1.0).astype(I32)
    nused_ref[...] = jnp.max(incl_b, axis=0, keepdims=True).astype(I32)


def _routing(logits_t, *, rows, n_blk, tr):
    e, t = logits_t.shape
    nbp = -(-n_blk // LANES) * LANES
    kern = functools.partial(_routing_kernel, tr=tr, rows=rows, n_blk=n_blk)
    return pl.pallas_call(
        kern,
        out_shape=[jax.ShapeDtypeStruct((TOP_K, t), I32), jax.ShapeDtypeStruct((TOP_K, t), F32),
                   jax.ShapeDtypeStruct((1, nbp), I32), jax.ShapeDtypeStruct((1, LANES), I32)],
        scratch_shapes=[pltpu.VMEM((TOP_K, t), F32), pltpu.VMEM((TOP_K, t), F32), pltpu.VMEM((e, LANES), F32)],
        compiler_params=pltpu.CompilerParams(vmem_limit_bytes=32 * 1024 * 1024),
        name="moe_routing",
    )(logits_t)


def _sc_mesh():
    return plsc.VectorSubcoreMesh(core_axis_name="c", subcore_axis_name="s",
                                  num_cores=V7X_SC_CORES, num_subcores=V7X_SC_SUBCORES)


def _sc_worker_id():
    return lax.axis_index("s") * V7X_SC_CORES + lax.axis_index("c")


def _dispatch_sc(dest_chunks, hp, *, n_pad):
    t, c = hp.shape
    per_worker = t // SC_CHUNK // (V7X_SC_CORES * V7X_SC_SUBCORES)

    @functools.partial(
        pl.kernel, mesh=_sc_mesh(), out_type=jax.ShapeDtypeStruct((n_pad, c), I32),
        scratch_types=[pltpu.VMEM((TOP_K, SC_CHUNK), I32), pltpu.VMEM((SC_CHUNK, c), I32), pltpu.SemaphoreType.DMA],
        name="moe_dispatch_sc")
    def scatter_rows(dest_hbm, hp_hbm, xs_hbm, idx_v, rows_v, sem):
        wid = _sc_worker_id()

        @pl.loop(0, per_worker)
        def _(i):
            chunk = wid * per_worker + i
            pltpu.sync_copy(dest_hbm.at[chunk], idx_v)
            pltpu.sync_copy(hp_hbm.at[pl.ds(chunk * SC_CHUNK, SC_CHUNK)], rows_v)
            copies = [pltpu.async_copy(rows_v, xs_hbm.at[idx_v.at[k]], sem) for k in range(TOP_K)]
            for cp in copies:
                cp.wait()

    return scatter_rows(dest_chunks, hp)


def _gather_sc(dest_chunks, ys, *, t):
    c = ys.shape[1]
    per_worker = t // SC_CHUNK // (V7X_SC_CORES * V7X_SC_SUBCORES)

    @functools.partial(
        pl.kernel, mesh=_sc_mesh(), out_type=jax.ShapeDtypeStruct((TOP_K, t, c), I32),
        scratch_types=[pltpu.VMEM((TOP_K, SC_CHUNK), I32), pltpu.VMEM((SC_CHUNK, c), I32), pltpu.SemaphoreType.DMA],
        name="moe_gather_sc")
    def gather_rows(dest_hbm, ys_hbm, out_hbm, idx_v, rows_v, sem):
        wid = _sc_worker_id()

        @pl.loop(0, per_worker)
        def _(i):
            chunk = wid * per_worker + i
            pltpu.sync_copy(dest_hbm.at[chunk], idx_v)
            for k in range(TOP_K):
                pltpu.async_copy(ys_hbm.at[idx_v.at[k]], rows_v, sem).wait()
                pltpu.sync_copy(rows_v, out_hbm.at[k, pl.ds(chunk * SC_CHUNK, SC_CHUNK)])

    return gather_rows(dest_chunks, ys)


def _expert_kernel(blke_ref, nused_ref, xs_ref, w1_ref, b1_ref, w2_ref, b2_ref, ys_ref, w1b_ref, w2b_ref):
    j = pl.program_id(0)

    @pl.when((j == 0) | (blke_ref[j] != blke_ref[jnp.maximum(j - 1, 0)]))
    def _():
        w1b_ref[...] = w1_ref[0].astype(BF16)
        w2b_ref[...] = w2_ref[0].astype(BF16)

    @pl.when(j < nused_ref[0])
    def _():
        f = w2_ref.shape[1]
        x = _unpack_bf16_pairs(xs_ref[...]).astype(BF16)
        hid = _dot(x, w1b_ref[...]) + b1_ref[0]
        gate = jnp.minimum(hid[:, :f], SWIGLU_LIMIT)
        lin = jnp.clip(hid[:, f:], -SWIGLU_LIMIT, SWIGLU_LIMIT)
        act = gate * jax.nn.sigmoid(SWIGLU_ALPHA * gate) * (lin + 1.0)
        y = _dot(act.astype(BF16), w2b_ref[...]) + b2_ref[0]
        ys_ref[...] = _pack_bf16_pairs(y)

    @pl.when(pl.program_id(0) >= nused_ref[0])
    def _():
        ys_ref[...] = jnp.zeros_like(ys_ref)


def _experts(blk_e, n_used, xs, w1, b1, w2, b2, *, rows):
    n_pad, c = xs.shape
    e, d, f2 = w1.shape
    f = f2 // 2
    n_blk = n_pad // rows
    row_blk = lambda j, be, nu: (jnp.minimum(j, nu[0] - 1), 0)
    wsel = lambda j, be, nu: (be[j], 0, 0)
    grid_spec = pltpu.PrefetchScalarGridSpec(
        num_scalar_prefetch=2,
        grid=(n_blk,),
        in_specs=[pl.BlockSpec((rows, c), row_blk),
                  pl.BlockSpec((1, d, f2), wsel), pl.BlockSpec((1, 1, f2), wsel),
                  pl.BlockSpec((1, f, d), wsel), pl.BlockSpec((1, 1, d), wsel)],
        out_specs=pl.BlockSpec((rows, c), lambda j, be, nu: (j, 0)),
        scratch_shapes=[pltpu.VMEM((d, f2), BF16), pltpu.VMEM((f, d), BF16)],
    )
    return pl.pallas_call(
        _expert_kernel,
        grid_spec=grid_spec,
        out_shape=jax.ShapeDtypeStruct((n_pad, c), U32),
        compiler_params=_cparams(("arbitrary",), 60),
        name="moe_experts",
    )(blk_e, n_used, xs, w1, b1, w2, b2)


def _combine_kernel(h_ref, gate_ref, g_ref, rows_ref, o_ref, *, final_norm):
    gates = gate_ref[...]
    out = h_ref[...]
    for k in range(TOP_K):
        out = out + _unpack_bf16_pairs(rows_ref[k]) * gates[:, k:k + 1]
    if final_norm:
        out = _rms(out, g_ref[...])
    o_ref[...] = out


def _combine(h, gates_t, g_final, rows, *, tm, final_norm):
    t, d = h.shape
    c = rows.shape[2]
    kern = functools.partial(_combine_kernel, final_norm=final_norm)
    return pl.pallas_call(
        kern,
        grid=(t // tm,),
        in_specs=[pl.BlockSpec((tm, d), lambda i: (i, 0)),
                  pl.BlockSpec((tm, TOP_K), lambda i: (i, 0)),
                  pl.BlockSpec((1, d), lambda i: (0, 0)),
                  pl.BlockSpec((TOP_K, tm, c), lambda i: (0, i, 0))],
        out_specs=pl.BlockSpec((tm, d), lambda i: (i, 0)),
        out_shape=jax.ShapeDtypeStruct((t, d), F32),
        compiler_params=_cparams(("parallel",), 32),
        name="moe_combine",
    )(h, gates_t, g_final, rows)


def _pick_tile(n, pref):
    t = min(n, pref)
    while n % t:
        t //= 2
    return t


def kernel(x, mem, norm_mix_g, w_in, lambda_q1, lambda_k1, lambda_q2, lambda_k2, subln_g, ssm_a_re, ssm_a_im, ssm_log_dt, ssm_b_re, ssm_b_im, ssm_c_re, ssm_c_im, ssm_d, w_glu, w_branch_attn, w_branch_ssm, w_mix_out, norm_cross_g, norm_mem_g, w_xq, w_xkv, w_xo, norm_ffn_g, w_router, b_router, w_e1, b_e1, w_e2, b_e2, norm_final_g):
    b, l, d = x.shape
    depth = w_in.shape[0]
    head_dim = lambda_q1.shape[-1]
    da_width = w_branch_attn.shape[1]
    heads = da_width // (2 * head_dim)
    ssm_width = w_branch_ssm.shape[1]
    n_exp = w_router.shape[-1]
    t = b * l
    assert d % (2 * LANES) == 0 and da_width % LANES == 0 and ssm_width % LANES == 0 and LANES % head_dim == 0
    assert head_dim & (head_dim - 1) == 0

    tl = _pick_tile(l, 512)
    tq = _pick_tile(l, 256)
    tc = _pick_tile(l, 64)
    tm = _pick_tile(t, 256)
    tr = _pick_tile(t, 512)
    rows = MOE_ROWS
    n_blk = (t * TOP_K) // rows + n_exp
    n_pad = n_blk * rows

    half = head_dim // 2
    inv = ROPE_THETA ** (-jnp.arange(half, dtype=F32) * (2.0 / head_dim))
    ang = jnp.arange(l, dtype=F32)[:, None] * inv[None, :]
    cos_t = jnp.tile(jnp.concatenate([jnp.cos(ang), jnp.cos(ang)], axis=1), (1, da_width // head_dim))
    sin_t = jnp.tile(jnp.concatenate([-jnp.sin(ang), jnp.sin(ang)], axis=1), (1, da_width // head_dim))

    row = lambda v: v.astype(F32).reshape(1, -1)
    h = x.astype(F32)
    out = None
    for layer in range(depth):
        lambda_init = 0.8 - 0.6 * math.exp(-0.3 * layer)
        q, k, v, u2, g_a, g_s = _inproj(h, row(norm_mix_g[layer]), w_in[layer].astype(BF16), cos_t, sin_t,
                                        da_width=da_width, ssm_width=ssm_width, head_dim=head_dim, tl=tl)
        lam_p = jnp.stack([lambda_q1[layer], lambda_k1[layer], lambda_q2[layer], lambda_k2[layer]]).astype(F32)
        o_a = _diffattn(lam_p, q, k, v, row(subln_g[layer]), heads=heads, head_dim=head_dim,
                        lambda_init=lambda_init, tq=tq)

        ab_re, ab_im, bb_re, bb_im = _s5_discretize(ssm_a_re[layer], ssm_a_im[layer], ssm_log_dt[layer],
                                                    ssm_b_re[layer], ssm_b_im[layer])
        o_s = _s5_mixer(u2.reshape(l, b, ssm_width), ab_re, ab_im, bb_re, bb_im,
                        ssm_c_re[layer].astype(F32), ssm_c_im[layer].astype(F32),
                        row(ssm_d[layer]), w_glu[layer].astype(BF16), tc=tc)

        kx, vx = _memkv(mem.astype(F32), row(norm_mem_g[layer]), w_xkv[layer].astype(BF16))
        wr_t = w_router[layer].astype(F32).T
        wr_hi = wr_t.astype(BF16)
        wr_lo = (wr_t - wr_hi.astype(F32)).astype(BF16)
        h2, hp, logits_t = _mix_cross(
            h, o_a, o_s.reshape(l, b * ssm_width), g_a, g_s,
            w_branch_attn[layer].astype(BF16), w_branch_ssm[layer].astype(BF16), w_mix_out[layer].astype(BF16),
            row(norm_cross_g[layer]), w_xq[layer].astype(BF16), kx, vx, w_xo[layer].astype(BF16),
            row(norm_ffn_g[layer]), wr_hi, wr_lo, b_router[layer].astype(F32).reshape(n_exp, 1), tl=tl)

        dest, gates, blk_e, n_used = _routing(logits_t, rows=rows, n_blk=n_blk, tr=tr)
        dest_chunks = dest.reshape(TOP_K, t // SC_CHUNK, SC_CHUNK).transpose(1, 0, 2)
        xs = lax.bitcast_convert_type(
            _dispatch_sc(dest_chunks, lax.bitcast_convert_type(hp, I32), n_pad=n_pad), U32)
        ys = _experts(blk_e[0, :n_blk], n_used[0, :1], xs,
                      w_e1[layer].astype(F32), b_e1[layer].astype(F32)[:, None, :],
                      w_e2[layer].astype(F32), b_e2[layer].astype(F32)[:, None, :], rows=rows)
        last = layer == depth - 1
        rows_k = lax.bitcast_convert_type(_gather_sc(dest_chunks, lax.bitcast_convert_type(ys, I32), t=t), U32)
        h_flat = _combine(h2.reshape(t, d), gates.T, row(norm_final_g), rows_k, tm=tm, final_norm=last)
        h = h_flat.reshape(b, l, d)
        out = h
    return out.astype(x.dtype)
```

```python
import functools
import math

import jax
import jax.numpy as jnp
from jax import lax
from jax.experimental import pallas as pl
from jax.experimental.pallas import tpu as pltpu
from jax.experimental.pallas import tpu_sc as plsc

X_HEADS = 4
TOP_K = 4
ROPE_THETA = 10000.0
SWIGLU_LIMIT = 7.0
SWIGLU_ALPHA = 1.702
EPS = 1e-6
LOG2E = 1.4426950408889634

LANES = 128
V7X_VMEM_BYTES = 64 * 1024 * 1024
V7X_SC_CORES = 2
V7X_SC_SUBCORES = 16
SC_CHUNK = 128

MOE_ROWS = 512

F32 = jnp.float32
BF16 = jnp.bfloat16
U32 = jnp.uint32
I32 = jnp.int32


def _cparams(semantics, vmem_mb):
    return pltpu.CompilerParams(dimension_semantics=semantics, vmem_limit_bytes=vmem_mb * 1024 * 1024)


def _rms(x, g):
    return x * lax.rsqrt(jnp.mean(x * x, axis=-1, keepdims=True) + EPS) * g


def _dot(a, b):
    return jnp.dot(a, b, preferred_element_type=F32)


def _dot_nt(a, b):
    return lax.dot_general(a, b, (((1,), (1,)), ((), ())), preferred_element_type=F32)


def _pack_bf16_pairs(x):
    c = x.shape[1] // 2
    lo = lax.bitcast_convert_type(x[:, :c].astype(BF16).astype(F32), U32)
    hi = lax.bitcast_convert_type(x[:, c:].astype(BF16).astype(F32), U32)
    return lax.bitcast_convert_type((hi & jnp.uint32(0xFFFF0000)) | (lo >> jnp.uint32(16)), I32)


def _unpack_bf16_pairs(w):
    w = lax.bitcast_convert_type(w, U32)
    lo = lax.bitcast_convert_type(w << jnp.uint32(16), F32)
    hi = lax.bitcast_convert_type(w & jnp.uint32(0xFFFF0000), F32)
    return jnp.concatenate([lo, hi], axis=1)


def _inproj_kernel(x_ref, g_ref, w_ref, cos_ref, sin_ref, q_ref, k_ref, v_ref, u_ref, ga_ref, gs_ref,
                   *, da_width, ssm_width, d_model, head_dim, q_scale):
    x = x_ref[0]
    hb = _rms(x, g_ref[...]).astype(BF16)
    tl = x.shape[0]
    half = head_dim // 2
    lane = lax.broadcasted_iota(I32, (tl, LANES), 1)
    first = (lane & (head_dim - 1)) < half

    def rope(z, scale):
        outs = []
        for c in range(da_width // LANES):
            zc = z[:, c * LANES:(c + 1) * LANES]
            sw = jnp.where(first, pltpu.roll(zc, LANES - half, 1), pltpu.roll(zc, half, 1))
            r = zc * cos_ref[:, c * LANES:(c + 1) * LANES] + sw * sin_ref[:, c * LANES:(c + 1) * LANES]
            outs.append(r * scale if scale != 1.0 else r)
        return jnp.concatenate(outs, axis=1)

    o = 0
    q_ref[0] = rope(_dot(hb, w_ref[:, o:o + da_width]), q_scale).astype(BF16)
    o += da_width
    k_ref[0] = rope(_dot(hb, w_ref[:, o:o + da_width]), 1.0).astype(BF16)
    o += da_width
    v_ref[0] = _dot(hb, w_ref[:, o:o + da_width]).astype(BF16)
    o += da_width
    u_ref[...] = _dot(hb, w_ref[:, o:o + ssm_width])
    o += ssm_width
    ga_ref[0] = _dot(hb, w_ref[:, o:o + d_model]).astype(BF16)
    o += d_model
    gs_ref[0] = _dot(hb, w_ref[:, o:o + d_model]).astype(BF16)


def _inproj(x, g, w_in, cos_t, sin_t, *, da_width, ssm_width, head_dim, tl):
    b, l, d = x.shape
    in_w = w_in.shape[1]
    kern = functools.partial(_inproj_kernel, da_width=da_width, ssm_width=ssm_width, d_model=d,
                             head_dim=head_dim, q_scale=head_dim ** -0.5 * LOG2E)
    tok = lambda width: pl.BlockSpec((1, tl, width), lambda i, j: (i, j, 0))
    return pl.pallas_call(
        kern,
        grid=(b, l // tl),
        in_specs=[tok(d),
                  pl.BlockSpec((1, d), lambda i, j: (0, 0)),
                  pl.BlockSpec((d, in_w), lambda i, j: (0, 0)),
                  pl.BlockSpec((tl, da_width), lambda i, j: (j, 0)),
                  pl.BlockSpec((tl, da_width), lambda i, j: (j, 0))],
        out_specs=[tok(da_width), tok(da_width), tok(da_width),
                   pl.BlockSpec((tl, ssm_width), lambda i, j: (j, i)),
                   tok(d), tok(d)],
        out_shape=[jax.ShapeDtypeStruct((b, l, da_width), BF16)] * 3
        + [jax.ShapeDtypeStruct((l, b * ssm_width), F32)]
        + [jax.ShapeDtypeStruct((b, l, d), BF16)] * 2,
        compiler_params=_cparams(("parallel", "parallel"), 48),
        name="inproj_rope",
    )(x, g, w_in, cos_t, sin_t)


def _diffattn_kernel(lam_ref, q_ref, k_ref, v_ref, g_ref, o_ref, *, tq, head_dim, lambda_init):
    l = q_ref.shape[1]
    vd = 2 * head_dim
    k = k_ref[0]
    v = v_ref[0]
    lp = lam_ref[...]
    lam = (jnp.exp(jnp.sum(lp[0:1] * lp[1:2], axis=-1, keepdims=True))
           - jnp.exp(jnp.sum(lp[2:3] * lp[3:4], axis=-1, keepdims=True)) + lambda_init)
    lane = lax.broadcasted_iota(I32, (tq, vd), 1)
    zero = jnp.zeros((), BF16)
    gain = g_ref[...] * (1.0 - lambda_init)

    def body(i, carry):
        r0 = pl.multiple_of(i * tq, tq)
        q = q_ref[0, pl.ds(r0, tq), :]
        def component(qm):
            s = _dot_nt(qm, k)
            p = jnp.exp2(s - jnp.max(s, axis=-1, keepdims=True))
            r = 1.0 / jnp.sum(p, axis=-1, keepdims=True)
            return _dot(p.astype(BF16), v) * r

        o = component(jnp.where(lane < head_dim, q, zero)) - lam * component(jnp.where(lane >= head_dim, q, zero))
        o_ref[0, pl.ds(r0, tq), :] = _rms(o, gain).astype(o_ref.dtype)
        return carry

    lax.fori_loop(0, l // tq, body, 0, unroll=4)


def _diffattn(lam_p, q, k, v, subln_g, *, heads, head_dim, lambda_init, tq):
    b, l, w = q.shape
    vd = 2 * head_dim
    kern = functools.partial(_diffattn_kernel, tq=tq, head_dim=head_dim, lambda_init=lambda_init)
    blk = pl.BlockSpec((1, l, vd), lambda i, h: (i, 0, h))
    return pl.pallas_call(
        kern,
        grid=(b, heads),
        in_specs=[pl.BlockSpec((4, head_dim), lambda i, h: (0, 0)), blk, blk, blk,
                  pl.BlockSpec((1, vd), lambda i, h: (0, 0))],
        out_specs=blk,
        out_shape=jax.ShapeDtypeStruct((b, l, w), BF16),
        compiler_params=_cparams(("parallel", "parallel"), 48),
        name="diff_attention",
    )(lam_p, q, k, v, subln_g)


def _s5_disc_kernel(are_ref, aim_ref, ldt_ref, bre_ref, bim_ref, abre_ref, abim_ref, bbre_ref, bbim_ref):
    for d in range(are_ref.shape[0]):
        a_re = are_ref[d]
        a_im = aim_ref[d]
        dt = jnp.exp(ldt_ref[d])
        mag = jnp.exp(a_re * dt)
        ang = a_im * dt
        ab_re = mag * jnp.cos(ang)
        ab_im = mag * jnp.sin(ang)
        den = a_re * a_re + a_im * a_im
        nr = ab_re - 1.0
        coef_re = (nr * a_re + ab_im * a_im) / den
        coef_im = (ab_im * a_re - nr * a_im) / den
        abre_ref[d] = ab_re
        abim_ref[d] = ab_im
        b_re = bre_ref[d]
        b_im = bim_ref[d]
        bbre_ref[d] = coef_re[None] * b_re - coef_im[None] * b_im
        bbim_ref[d] = coef_re[None] * b_im + coef_im[None] * b_re


def _s5_discretize(a_re, a_im, log_dt, b_re, b_im):
    two, g, p, hg = b_re.shape
    bt_re = jnp.transpose(b_re, (0, 3, 1, 2))
    bt_im = jnp.transpose(b_im, (0, 3, 1, 2))
    return pl.pallas_call(
        _s5_disc_kernel,
        out_shape=[jax.ShapeDtypeStruct((two, g, p), F32)] * 2 + [jax.ShapeDtypeStruct((two, hg, g, p), F32)] * 2,
        name="s5_discretize",
    )(a_re.astype(F32), a_im.astype(F32), log_dt.astype(F32)[..., None], bt_re.astype(F32), bt_im.astype(F32))


def _block_diag(m, gb):
    g, r, c = m.shape
    mb = m.reshape(g // gb, gb, r, c)
    eye = jnp.eye(gb, dtype=m.dtype)
    return jnp.einsum('jgrc,gh->jgrhc', mb, eye).reshape(g // gb, gb * r, gb * c)


def _s5_scan_chunk(u_ref, bre_ref, bim_ref, are_ref, aim_ref, cre_ref, cim_ref, sre_ref, sim_ref, bufre, bufim,
                   *, reverse):
    tc, nb, w = u_ref.shape
    nbund = w // LANES
    lw = are_ref.shape[-1]
    ys = []
    for j in range(nbund):
        ub = u_ref[:, :, j * LANES:(j + 1) * LANES].reshape(tc * nb, LANES).astype(BF16)
        bufre[...] = _dot(ub, bre_ref[j])
        bufim[...] = _dot(ub, bim_ref[j])
        a_re = jnp.broadcast_to(are_ref[j], (nb, lw))
        a_im = jnp.broadcast_to(aim_ref[j], (nb, lw))

        def step(t, carry):
            xr, xi = carry
            tt = (tc - 1 - t) if reverse else t
            rows = pl.ds(pl.multiple_of(tt * nb, nb), nb)
            nr = a_re * xr - a_im * xi + bufre[rows, :]
            ni = a_re * xi + a_im * xr + bufim[rows, :]
            bufre[rows, :] = nr
            bufim[rows, :] = ni
            return nr, ni

        xr, xi = lax.fori_loop(0, tc, step, (sre_ref[j], sim_ref[j]), unroll=4)
        sre_ref[j] = xr
        sim_ref[j] = xi
        ys.append(_dot(bufre[...].astype(BF16), cre_ref[j]) - _dot(bufim[...].astype(BF16), cim_ref[j]))
    return ys


def _s5_fwd_kernel(u_ref, bre_ref, bim_ref, are_ref, aim_ref, cre_ref, cim_ref, y_ref, sre_ref, sim_ref,
                   bufre, bufim):
    @pl.when(pl.program_id(0) == 0)
    def _():
        sre_ref[...] = jnp.zeros_like(sre_ref)
        sim_ref[...] = jnp.zeros_like(sim_ref)

    tc, nb, _ = u_ref.shape
    ys = _s5_scan_chunk(u_ref, bre_ref, bim_ref, are_ref, aim_ref, cre_ref, cim_ref, sre_ref, sim_ref,
                        bufre, bufim, reverse=False)
    for j, y in enumerate(ys):
        y_ref[:, :, j * LANES:(j + 1) * LANES] = y.reshape(tc, nb, LANES)


def _s5_bwd_kernel(u_ref, yf_ref, bre_ref, bim_ref, are_ref, aim_ref, cre_ref, cim_ref, d_ref, wglu_ref, o_ref,
                   sre_ref, sim_ref, bufre, bufim):
    @pl.when(pl.program_id(0) == 0)
    def _():
        sre_ref[...] = jnp.zeros_like(sre_ref)
        sim_ref[...] = jnp.zeros_like(sim_ref)

    tc, nb, w = u_ref.shape
    ys = _s5_scan_chunk(u_ref, bre_ref, bim_ref, are_ref, aim_ref, cre_ref, cim_ref, sre_ref, sim_ref,
                        bufre, bufim, reverse=True)
    y = jnp.concatenate(ys, axis=1)
    y = y + yf_ref[...].reshape(tc * nb, w) + d_ref[...] * u_ref[...].reshape(tc * nb, w)
    y = jax.nn.gelu(y).astype(BF16)
    vg = _dot(y, wglu_ref[...])
    o = vg[:, :w] * jax.nn.sigmoid(vg[:, w:])
    o_ref[...] = o.reshape(tc, nb, w).astype(o_ref.dtype)


def _s5_mixer(u3, ab_re, ab_im, bb_re, bb_im, c_re, c_im, d_skip, w_glu, *, tc):
    l, nb, w = u3.shape
    two, hg, g, p = bb_re.shape
    gb = LANES // hg
    nbund = g // gb
    lw = gb * p
    nch = l // tc

    def direction_params(d):
        bre = _block_diag(jnp.swapaxes(bb_re[d], 0, 1), gb).astype(BF16)
        bim = _block_diag(jnp.swapaxes(bb_im[d], 0, 1), gb).astype(BF16)
        cre = _block_diag(jnp.swapaxes(c_re[d], 1, 2), gb).astype(BF16)
        cim = _block_diag(jnp.swapaxes(c_im[d], 1, 2), gb).astype(BF16)
        are = ab_re[d].reshape(nbund, 1, lw)
        aim = ab_im[d].reshape(nbund, 1, lw)
        return bre, bim, are, aim, cre, cim

    full = lambda a: pl.BlockSpec(a.shape, lambda i: (0,) * a.ndim)
    scratch = [pltpu.VMEM((nbund, nb, lw), F32), pltpu.VMEM((nbund, nb, lw), F32),
               pltpu.VMEM((tc * nb, lw), F32), pltpu.VMEM((tc * nb, lw), F32)]

    pf = direction_params(0)
    y_f = pl.pallas_call(
        _s5_fwd_kernel,
        grid=(nch,),
        in_specs=[pl.BlockSpec((tc, nb, w), lambda i: (i, 0, 0))] + [full(a) for a in pf],
        out_specs=pl.BlockSpec((tc, nb, w), lambda i: (i, 0, 0)),
        out_shape=jax.ShapeDtypeStruct((l, nb, w), F32),
        scratch_shapes=scratch,
        compiler_params=_cparams(("arbitrary",), 48),
        name="s5_forward_scan",
    )(u3, *pf)

    pb = direction_params(1)
    rev = lambda i: (nch - 1 - i, 0, 0)
    return pl.pallas_call(
        _s5_bwd_kernel,
        grid=(nch,),
        in_specs=[pl.BlockSpec((tc, nb, w), rev), pl.BlockSpec((tc, nb, w), rev)] + [full(a) for a in pb]
        + [pl.BlockSpec((1, w), lambda i: (0, 0)), pl.BlockSpec(w_glu.shape, lambda i: (0, 0))],
        out_specs=pl.BlockSpec((tc, nb, w), rev),
        out_shape=jax.ShapeDtypeStruct((l, nb, w), BF16),
        scratch_shapes=scratch,
        compiler_params=_cparams(("arbitrary",), 48),
        name="s5_backward_scan_glu",
    )(u3, y_f, *pb, d_skip, w_glu)


def _memkv_kernel(m_ref, g_ref, w_ref, k_ref, v_ref):
    d = m_ref.shape[2]
    mn = _rms(m_ref[0], g_ref[...]).astype(BF16)
    k_ref[0] = _dot(mn, w_ref[:, :d]).astype(BF16)
    v_ref[0] = _dot(mn, w_ref[:, d:]).astype(BF16)


def _memkv(mem, g, w_xkv):
    b, n, d = mem.shape
    blk = pl.BlockSpec((1, n, d), lambda i: (i, 0, 0))
    return pl.pallas_call(
        _memkv_kernel,
        grid=(b,),
        in_specs=[blk, pl.BlockSpec((1, d), lambda i: (0, 0)), pl.BlockSpec((d, 2 * d), lambda i: (0, 0))],
        out_specs=[blk, blk],
        out_shape=[jax.ShapeDtypeStruct((b, n, d), BF16)] * 2,
        compiler_params=_cparams(("parallel",), 32),
        name="mem_kv_proj",
    )(mem, g, w_xkv)


def _mix_cross_kernel(x_ref, oa_ref, os_ref, ga_ref, gs_ref, wba_ref, wbs_ref, wmo_ref, gc_ref, wxq_ref,
                      kx_ref, vx_ref, wxo_ref, gf_ref, wrh_ref, wrl_ref, br_ref,
                      h_ref, hp_ref, lg_ref, *, heads):
    d = x_ref.shape[2]
    hd = d // heads
    merged = (jax.nn.sigmoid(ga_ref[0].astype(F32)) * _dot(oa_ref[0], wba_ref[...])
              + jax.nn.sigmoid(gs_ref[0].astype(F32)) * _dot(os_ref[...], wbs_ref[...]))
    h1 = x_ref[0] + _dot(merged.astype(BF16), wmo_ref[...])

    qx = (_dot(_rms(h1, gc_ref[...]).astype(BF16), wxq_ref[...]) * (hd ** -0.5 * LOG2E)).astype(BF16)
    outs = []
    for hh in range(heads):
        sl = slice(hh * hd, (hh + 1) * hd)
        s = _dot_nt(qx[:, sl], kx_ref[0, :, sl])
        p = jnp.exp2(s - jnp.max(s, axis=-1, keepdims=True))
        p = p * (1.0 / jnp.sum(p, axis=-1, keepdims=True))
        outs.append(_dot(p.astype(BF16), vx_ref[0, :, sl]))
    h2 = h1 + _dot(jnp.concatenate(outs, axis=1).astype(BF16), wxo_ref[...])
    h_ref[0] = h2

    hn = _rms(h2, gf_ref[...])
    hp_ref[...] = _pack_bf16_pairs(hn)
    hi = hn.astype(BF16)
    lo = (hn - hi.astype(F32)).astype(BF16)
    lg_ref[...] = (_dot_nt(wrh_ref[...], hi) + _dot_nt(wrh_ref[...], lo) + _dot_nt(wrl_ref[...], hi)) + br_ref[...]


def _mix_cross(x, o_a, o_s2, g_a, g_s, wba, wbs, wmo, gc, wxq, kx, vx, wxo, gf, wr_hi, wr_lo, b_r, *, tl):
    b, l, d = x.shape
    aw = o_a.shape[2]
    sw = o_s2.shape[1] // b
    n_mem = kx.shape[1]
    e = wr_hi.shape[0]
    nt = l // tl
    kern = functools.partial(_mix_cross_kernel, heads=X_HEADS)
    tok = lambda width: pl.BlockSpec((1, tl, width), lambda i, j: (i, j, 0))
    full = lambda a: pl.BlockSpec(a.shape, lambda i, j: (0,) * a.ndim)
    return pl.pallas_call(
        kern,
        grid=(b, nt),
        in_specs=[tok(d), tok(aw), pl.BlockSpec((tl, sw), lambda i, j: (j, i)), tok(d), tok(d),
                  full(wba), full(wbs), full(wmo), full(gc), full(wxq),
                  pl.BlockSpec((1, n_mem, d), lambda i, j: (i, 0, 0)),
                  pl.BlockSpec((1, n_mem, d), lambda i, j: (i, 0, 0)),
                  full(wxo), full(gf), full(wr_hi), full(wr_lo), full(b_r)],
        out_specs=[tok(d),
                   pl.BlockSpec((tl, d // 2), lambda i, j: (i * nt + j, 0)),
                   pl.BlockSpec((e, tl), lambda i, j: (0, i * nt + j))],
        out_shape=[jax.ShapeDtypeStruct((b, l, d), F32),
                   jax.ShapeDtypeStruct((b * l, d // 2), I32),
                   jax.ShapeDtypeStruct((e, b * l), F32)],
        compiler_params=_cparams(("parallel", "parallel"), 56),
        name="mix_cross_router",
    )(x, o_a, o_s2, g_a, g_s, wba, wbs, wmo, gc, wxq, kx, vx, wxo, gf, wr_hi, wr_lo, b_r)


def _routing_kernel(lg_ref, dest_ref, gate_ref, blke_ref, nused_ref, idx_s, rank_s, cnt_s, *, tr, rows, n_blk):
    e, t = lg_ref.shape
    nt = t // tr
    ie = lax.broadcasted_iota(I32, (e, tr), 0).astype(F32)
    tri = (lax.broadcasted_iota(I32, (tr, tr), 0) <= lax.broadcasted_iota(I32, (tr, tr), 1)).astype(BF16)
    cnt_s[...] = jnp.zeros_like(cnt_s)

    def phase1(i, carry):
        cols = pl.ds(pl.multiple_of(i * tr, tr), tr)
        v = lg_ref[:, cols]
        tops, hots = [], []
        for k in range(TOP_K):
            m = jnp.max(v, axis=0, keepdims=True)
            idx = jnp.min(jnp.where(v == m, ie, float(e)), axis=0, keepdims=True)
            hot = ie == idx
            v = jnp.where(hot, -jnp.inf, v)
            tops.append(m)
            hots.append(hot)
            idx_s[k:k + 1, cols] = idx
        ex = [jnp.exp(m - tops[0]) for m in tops]
        den = ex[0] + ex[1] + ex[2] + ex[3]
        for k in range(TOP_K):
            gate_ref[k:k + 1, cols] = ex[k] / den
        hot_all = hots[0] | hots[1] | hots[2] | hots[3]
        hot_f = jnp.where(hot_all, 1.0, 0.0)
        incl = _dot(hot_f.astype(BF16), tri)
        before = cnt_s[:, 0:1] + incl - hot_f
        for k in range(TOP_K):
            rank_s[k:k + 1, cols] = jnp.sum(jnp.where(hots[k], before, 0.0), axis=0, keepdims=True)
        cnt_s[...] = cnt_s[...] + jnp.sum(hot_f, axis=1, keepdims=True)
        return carry

    lax.fori_loop(0, nt, phase1, 0)

    cnt = cnt_s[...]
    nblk_e = jnp.floor((cnt + (rows - 1.0)) * (1.0 / rows))
    row = lax.broadcasted_iota(I32, cnt.shape, 0)
    incl_b = nblk_e
    s = 1
    while s < e:
        incl_b = incl_b + jnp.where(row >= s, pltpu.roll(incl_b, s, 0), 0.0)
        s *= 2
    start_rows = (incl_b - nblk_e) * float(rows)
    start_col = start_rows[:, 0:1]

    def phase2(i, carry):
        cols = pl.ds(pl.multiple_of(i * tr, tr), tr)
        for k in range(TOP_K):
            hot = ie == idx_s[k:k + 1, cols]
            base = jnp.sum(jnp.where(hot, start_col, 0.0), axis=0, keepdims=True)
            dest_ref[k:k + 1, cols] = (base + rank_s[k:k + 1, cols]).astype(I32)
        return carry

    lax.fori_loop(0, nt, phase2, 0)

    nbp = blke_ref.shape[1]
    jb = lax.broadcasted_iota(I32, (e, nbp), 1).astype(F32)
    ends = incl_b[:, 0:1]
    be = jnp.sum(jnp.where(ends <= jb, 1.0, 0.0), axis=0, keepdims=True)
    blke_ref[...] = jnp.minimum(be, e - 1.0).astype(I32)
    nused_ref[...] = jnp.max(incl_b, axis=0, keepdims=True).astype(I32)


def _routing(logits_t, *, rows, n_blk, tr):
    e, t = logits_t.shape
    nbp = -(-n_blk // LANES) * LANES
    kern = functools.partial(_routing_kernel, tr=tr, rows=rows, n_blk=n_blk)
    return pl.pallas_call(
        kern,
        out_shape=[jax.ShapeDtypeStruct((TOP_K, t), I32), jax.ShapeDtypeStruct((TOP_K, t), F32),
                   jax.ShapeDtypeStruct((1, nbp), I32), jax.ShapeDtypeStruct((1, LANES), I32)],
        scratch_shapes=[pltpu.VMEM((TOP_K, t), F32), pltpu.VMEM((TOP_K, t), F32), pltpu.VMEM((e, LANES), F32)],
        compiler_params=pltpu.CompilerParams(vmem_limit_bytes=32 * 1024 * 1024),
        name="moe_routing",
    )(logits_t)


def _sc_mesh():
    return plsc.VectorSubcoreMesh(core_axis_name="c", subcore_axis_name="s",
                                  num_cores=V7X_SC_CORES, num_subcores=V7X_SC_SUBCORES)


def _sc_worker_id():
    return lax.axis_index("s") * V7X_SC_CORES + lax.axis_index("c")


def _dispatch_sc(dest_chunks, hp, *, n_pad):
    t, c = hp.shape
    per_worker = t // SC_CHUNK // (V7X_SC_CORES * V7X_SC_SUBCORES)

    @functools.partial(
        pl.kernel, mesh=_sc_mesh(), out_type=jax.ShapeDtypeStruct((n_pad, c), I32),
        scratch_types=[pltpu.VMEM((TOP_K, SC_CHUNK), I32), pltpu.VMEM((SC_CHUNK, c), I32), pltpu.SemaphoreType.DMA],
        name="moe_dispatch_sc")
    def scatter_rows(dest_hbm, hp_hbm, xs_hbm, idx_v, rows_v, sem):
        wid = _sc_worker_id()

        @pl.loop(0, per_worker)
        def _(i):
            chunk = wid * per_worker + i
            pltpu.sync_copy(dest_hbm.at[chunk], idx_v)
            pltpu.sync_copy(hp_hbm.at[pl.ds(chunk * SC_CHUNK, SC_CHUNK)], rows_v)
            copies = [pltpu.async_copy(rows_v, xs_hbm.at[idx_v.at[k]], sem) for k in range(TOP_K)]
            for cp in copies:
                cp.wait()

    return scatter_rows(dest_chunks, hp)


def _gather_sc(dest_chunks, ys, *, t):
    c = ys.shape[1]
    per_worker = t // SC_CHUNK // (V7X_SC_CORES * V7X_SC_SUBCORES)

    @functools.partial(
        pl.kernel, mesh=_sc_mesh(), out_type=jax.ShapeDtypeStruct((TOP_K, t, c), I32),
        scratch_types=[pltpu.VMEM((TOP_K, SC_CHUNK), I32), pltpu.VMEM((SC_CHUNK, c), I32), pltpu.SemaphoreType.DMA],
        name="moe_gather_sc")
    def gather_rows(dest_hbm, ys_hbm, out_hbm, idx_v, rows_v, sem):
        wid = _sc_worker_id()

        @pl.loop(0, per_worker)
        def _(i):
            chunk = wid * per_worker + i
            pltpu.sync_copy(dest_hbm.at[chunk], idx_v)
            for k in range(TOP_K):
                pltpu.async_copy(ys_hbm.at[idx_v.at[k]], rows_v, sem).wait()
                pltpu.sync_copy(rows_v, out_hbm.at[k, pl.ds(chunk * SC_CHUNK, SC_CHUNK)])

    return gather_rows(dest_chunks, ys)


def _expert_kernel(blke_ref, nused_ref, xs_ref, w1_ref, b1_ref, w2_ref, b2_ref, ys_ref, w1b_ref, w2b_ref):
    j = pl.program_id(0)

    @pl.when((j == 0) | (blke_ref[j] != blke_ref[jnp.maximum(j - 1, 0)]))
    def _():
        w1b_ref[...] = w1_ref[0].astype(BF16)
        w2b_ref[...] = w2_ref[0].astype(BF16)

    @pl.when(j < nused_ref[0])
    def _():
        f = w2_ref.shape[1]
        x = _unpack_bf16_pairs(xs_ref[...]).astype(BF16)
        hid = _dot(x, w1b_ref[...]) + b1_ref[0]
        gate = jnp.minimum(hid[:, :f], SWIGLU_LIMIT)
        lin = jnp.clip(hid[:, f:], -SWIGLU_LIMIT, SWIGLU_LIMIT)
        act = gate * jax.nn.sigmoid(SWIGLU_ALPHA * gate) * (lin + 1.0)
        y = _dot(act.astype(BF16), w2b_ref[...]) + b2_ref[0]
        ys_ref[...] = _pack_bf16_pairs(y)

    @pl.when(pl.program_id(0) >= nused_ref[0])
    def _():
        ys_ref[...] = jnp.zeros_like(ys_ref)


def _experts(blk_e, n_used, xs, w1, b1, w2, b2, *, rows):
    n_pad, c = xs.shape
    e, d, f2 = w1.shape
    f = f2 // 2
    n_blk = n_pad // rows
    row_blk = lambda j, be, nu: (jnp.minimum(j, nu[0] - 1), 0)
    wsel = lambda j, be, nu: (be[j], 0, 0)
    grid_spec = pltpu.PrefetchScalarGridSpec(
        num_scalar_prefetch=2,
        grid=(n_blk,),
        in_specs=[pl.BlockSpec((rows, c), row_blk),
                  pl.BlockSpec((1, d, f2), wsel), pl.BlockSpec((1, 1, f2), wsel),
                  pl.BlockSpec((1, f, d), wsel), pl.BlockSpec((1, 1, d), wsel)],
        out_specs=pl.BlockSpec((rows, c), lambda j, be, nu: (j, 0)),
        scratch_shapes=[pltpu.VMEM((d, f2), BF16), pltpu.VMEM((f, d), BF16)],
    )
    return pl.pallas_call(
        _expert_kernel,
        grid_spec=grid_spec,
        out_shape=jax.ShapeDtypeStruct((n_pad, c), I32),
        compiler_params=_cparams(("arbitrary",), 60),
        name="moe_experts",
    )(blk_e, n_used, xs, w1, b1, w2, b2)


def _combine_kernel(h_ref, gate_ref, g_ref, rows_ref, o_ref, *, final_norm):
    gates = gate_ref[...]
    out = h_ref[...]
    for k in range(TOP_K):
        out = out + _unpack_bf16_pairs(rows_ref[k]) * gates[:, k:k + 1]
    if final_norm:
        out = _rms(out, g_ref[...])
    o_ref[...] = out


def _combine(h, gates_t, g_final, rows, *, tm, final_norm):
    t, d = h.shape
    c = rows.shape[2]
    kern = functools.partial(_combine_kernel, final_norm=final_norm)
    return pl.pallas_call(
        kern,
        grid=(t // tm,),
        in_specs=[pl.BlockSpec((tm, d), lambda i: (i, 0)),
                  pl.BlockSpec((tm, TOP_K), lambda i: (i, 0)),
                  pl.BlockSpec((1, d), lambda i: (0, 0)),
                  pl.BlockSpec((TOP_K, tm, c), lambda i: (0, i, 0))],
        out_specs=pl.BlockSpec((tm, d), lambda i: (i, 0)),
        out_shape=jax.ShapeDtypeStruct((t, d), F32),
        compiler_params=_cparams(("parallel",), 32),
        name="moe_combine",
    )(h, gates_t, g_final, rows)


def _pick_tile(n, pref):
    t = min(n, pref)
    while n % t:
        t //= 2
    return t


def kernel(x, mem, norm_mix_g, w_in, lambda_q1, lambda_k1, lambda_q2, lambda_k2, subln_g, ssm_a_re, ssm_a_im, ssm_log_dt, ssm_b_re, ssm_b_im, ssm_c_re, ssm_c_im, ssm_d, w_glu, w_branch_attn, w_branch_ssm, w_mix_out, norm_cross_g, norm_mem_g, w_xq, w_xkv, w_xo, norm_ffn_g, w_router, b_router, w_e1, b_e1, w_e2, b_e2, norm_final_g):
    b, l, d = x.shape
    depth = w_in.shape[0]
    head_dim = lambda_q1.shape[-1]
    da_width = w_branch_attn.shape[1]
    heads = da_width // (2 * head_dim)
    ssm_width = w_branch_ssm.shape[1]
    n_exp = w_router.shape[-1]
    t = b * l
    assert d % (2 * LANES) == 0 and da_width % LANES == 0 and ssm_width % LANES == 0 and LANES % head_dim == 0
    assert head_dim & (head_dim - 1) == 0

    tl = _pick_tile(l, 512)
    tq = _pick_tile(l, 256)
    tc = _pick_tile(l, 64)
    tm = _pick_tile(t, 256)
    tr = _pick_tile(t, 512)
    rows = MOE_ROWS
    n_blk = (t * TOP_K) // rows + n_exp
    n_pad = n_blk * rows

    half = head_dim // 2
    inv = ROPE_THETA ** (-jnp.arange(half, dtype=F32) * (2.0 / head_dim))
    ang = jnp.arange(l, dtype=F32)[:, None] * inv[None, :]
    cos_t = jnp.tile(jnp.concatenate([jnp.cos(ang), jnp.cos(ang)], axis=1), (1, da_width // head_dim))
    sin_t = jnp.tile(jnp.concatenate([-jnp.sin(ang), jnp.sin(ang)], axis=1), (1, da_width // head_dim))

    row = lambda v: v.astype(F32).reshape(1, -1)
    h = x.astype(F32)
    out = None
    for layer in range(depth):
        lambda_init = 0.8 - 0.6 * math.exp(-0.3 * layer)
        q, k, v, u2, g_a, g_s = _inproj(h, row(norm_mix_g[layer]), w_in[layer].astype(BF16), cos_t, sin_t,
                                        da_width=da_width, ssm_width=ssm_width, head_dim=head_dim, tl=tl)
        lam_p = jnp.stack([lambda_q1[layer], lambda_k1[layer], lambda_q2[layer], lambda_k2[layer]]).astype(F32)
        o_a = _diffattn(lam_p, q, k, v, row(subln_g[layer]), heads=heads, head_dim=head_dim,
                        lambda_init=lambda_init, tq=tq)

        ab_re, ab_im, bb_re, bb_im = _s5_discretize(ssm_a_re[layer], ssm_a_im[layer], ssm_log_dt[layer],
                                                    ssm_b_re[layer], ssm_b_im[layer])
        o_s = _s5_mixer(u2.reshape(l, b, ssm_width), ab_re, ab_im, bb_re, bb_im,
                        ssm_c_re[layer].astype(F32), ssm_c_im[layer].astype(F32),
                        row(ssm_d[layer]), w_glu[layer].astype(BF16), tc=tc)

        kx, vx = _memkv(mem.astype(F32), row(norm_mem_g[layer]), w_xkv[layer].astype(BF16))
        wr_t = w_router[layer].astype(F32).T
        wr_hi = wr_t.astype(BF16)
        wr_lo = (wr_t - wr_hi.astype(F32)).astype(BF16)
        h2, hp, logits_t = _mix_cross(
            h, o_a, o_s.reshape(l, b * ssm_width), g_a, g_s,
            w_branch_attn[layer].astype(BF16), w_branch_ssm[layer].astype(BF16), w_mix_out[layer].astype(BF16),
            row(norm_cross_g[layer]), w_xq[layer].astype(BF16), kx, vx, w_xo[layer].astype(BF16),
            row(norm_ffn_g[layer]), wr_hi, wr_lo, b_router[layer].astype(F32).reshape(n_exp, 1), tl=tl)

        dest, gates, blk_e, n_used = _routing(logits_t, rows=rows, n_blk=n_blk, tr=tr)
        dest_chunks = dest.reshape(TOP_K, t // SC_CHUNK, SC_CHUNK).transpose(1, 0, 2)
        xs = _dispatch_sc(dest_chunks, hp, n_pad=n_pad)
        ys = _experts(blk_e[0, :n_blk], n_used[0, :1], xs,
                      w_e1[layer].astype(F32), b_e1[layer].astype(F32)[:, None, :],
                      w_e2[layer].astype(F32), b_e2[layer].astype(F32)[:, None, :], rows=rows)
        last = layer == depth - 1
        rows_k = _gather_sc(dest_chunks, ys, t=t)
        h_flat = _combine(h2.reshape(t, d), gates.T, row(norm_final_g), rows_k, tm=tm, final_norm=last)
        h = h_flat.reshape(b, l, d)
        out = h
    return out.astype(x.dtype)
```

```python
import functools
import math

import jax
import jax.numpy as jnp
from jax import lax
from jax.experimental import pallas as pl
from jax.experimental.pallas import tpu as pltpu
from jax.experimental.pallas import tpu_sc as plsc

X_HEADS = 4
TOP_K = 4
ROPE_THETA = 10000.0
SWIGLU_LIMIT = 7.0
SWIGLU_ALPHA = 1.702
EPS = 1e-6
LOG2E = 1.4426950408889634

LANES = 128
V7X_VMEM_BYTES = 64 * 1024 * 1024
V7X_SC_CORES = 2
V7X_SC_SUBCORES = 16
SC_CHUNK = 128

MOE_ROWS = 512
S5_CHUNK = 16

F32 = jnp.float32
BF16 = jnp.bfloat16
U32 = jnp.uint32
I32 = jnp.int32


def _cparams(semantics, vmem_mb):
    return pltpu.CompilerParams(dimension_semantics=semantics, vmem_limit_bytes=vmem_mb * 1024 * 1024)


def _rms(x, g):
    return x * lax.rsqrt(jnp.mean(x * x, axis=-1, keepdims=True) + EPS) * g


def _dot(a, b):
    return jnp.dot(a, b, preferred_element_type=F32)


def _dot_nt(a, b):
    return lax.dot_general(a, b, (((1,), (1,)), ((), ())), preferred_element_type=F32)


def _pack_bf16_pairs(x):
    c = x.shape[1] // 2
    lo = lax.bitcast_convert_type(x[:, :c].astype(BF16).astype(F32), U32)
    hi = lax.bitcast_convert_type(x[:, c:].astype(BF16).astype(F32), U32)
    return lax.bitcast_convert_type((hi & jnp.uint32(0xFFFF0000)) | (lo >> jnp.uint32(16)), I32)


def _unpack_bf16_pairs(w):
    w = lax.bitcast_convert_type(w, U32)
    lo = lax.bitcast_convert_type(w << jnp.uint32(16), F32)
    hi = lax.bitcast_convert_type(w & jnp.uint32(0xFFFF0000), F32)
    return jnp.concatenate([lo, hi], axis=1)


def _inproj_kernel(x_ref, g_ref, w_ref, cos_ref, sin_ref, q_ref, k_ref, v_ref, u_ref, ga_ref, gs_ref,
                   *, da_width, ssm_width, d_model, head_dim, q_scale):
    x = x_ref[0]
    hb = _rms(x, g_ref[...]).astype(BF16)
    tl = x.shape[0]
    half = head_dim // 2
    lane = lax.broadcasted_iota(I32, (tl, LANES), 1)
    first = (lane & (head_dim - 1)) < half

    def rope(z, scale):
        outs = []
        for c in range(da_width // LANES):
            zc = z[:, c * LANES:(c + 1) * LANES]
            sw = jnp.where(first, pltpu.roll(zc, LANES - half, 1), pltpu.roll(zc, half, 1))
            r = zc * cos_ref[:, c * LANES:(c + 1) * LANES] + sw * sin_ref[:, c * LANES:(c + 1) * LANES]
            outs.append(r * scale if scale != 1.0 else r)
        return jnp.concatenate(outs, axis=1)

    o = 0
    q_ref[0] = rope(_dot(hb, w_ref[:, o:o + da_width]), q_scale).astype(BF16)
    o += da_width
    k_ref[0] = rope(_dot(hb, w_ref[:, o:o + da_width]), 1.0).astype(BF16)
    o += da_width
    v_ref[0] = _dot(hb, w_ref[:, o:o + da_width]).astype(BF16)
    o += da_width
    u_ref[0] = _dot(hb, w_ref[:, o:o + ssm_width])
    o += ssm_width
    ga_ref[0] = _dot(hb, w_ref[:, o:o + d_model]).astype(BF16)
    o += d_model
    gs_ref[0] = _dot(hb, w_ref[:, o:o + d_model]).astype(BF16)


def _inproj(x, g, w_in, cos_t, sin_t, *, da_width, ssm_width, head_dim, tl):
    b, l, d = x.shape
    in_w = w_in.shape[1]
    kern = functools.partial(_inproj_kernel, da_width=da_width, ssm_width=ssm_width, d_model=d,
                             head_dim=head_dim, q_scale=head_dim ** -0.5 * LOG2E)
    tok = lambda width: pl.BlockSpec((1, tl, width), lambda i, j: (i, j, 0))
    return pl.pallas_call(
        kern,
        grid=(b, l // tl),
        in_specs=[tok(d),
                  pl.BlockSpec((1, d), lambda i, j: (0, 0)),
                  pl.BlockSpec((d, in_w), lambda i, j: (0, 0)),
                  pl.BlockSpec((tl, da_width), lambda i, j: (j, 0)),
                  pl.BlockSpec((tl, da_width), lambda i, j: (j, 0))],
        out_specs=[tok(da_width), tok(da_width), tok(da_width), tok(ssm_width), tok(d), tok(d)],
        out_shape=[jax.ShapeDtypeStruct((b, l, da_width), BF16)] * 3
        + [jax.ShapeDtypeStruct((b, l, ssm_width), F32)]
        + [jax.ShapeDtypeStruct((b, l, d), BF16)] * 2,
        compiler_params=_cparams(("parallel", "parallel"), 48),
        name="inproj_rope",
    )(x, g, w_in, cos_t, sin_t)


def _diffattn_kernel(lam_ref, q_ref, k_ref, v_ref, g_ref, o_ref, *, tq, head_dim, lambda_init):
    l = q_ref.shape[1]
    vd = 2 * head_dim
    k = k_ref[0]
    v = v_ref[0]
    lp = lam_ref[...]
    lam = (jnp.exp(jnp.sum(lp[0:1] * lp[1:2], axis=-1, keepdims=True))
           - jnp.exp(jnp.sum(lp[2:3] * lp[3:4], axis=-1, keepdims=True)) + lambda_init)
    lane = lax.broadcasted_iota(I32, (tq, vd), 1)
    zero = jnp.zeros((), BF16)
    gain = g_ref[...] * (1.0 - lambda_init)

    def body(i, carry):
        r0 = pl.multiple_of(i * tq, tq)
        q = q_ref[0, pl.ds(r0, tq), :]
        def component(qm):
            s = _dot_nt(qm, k)
            p = jnp.exp2(s - jnp.max(s, axis=-1, keepdims=True))
            r = 1.0 / jnp.sum(p, axis=-1, keepdims=True)
            return _dot(p.astype(BF16), v) * r

        o = component(jnp.where(lane < head_dim, q, zero)) - lam * component(jnp.where(lane >= head_dim, q, zero))
        o_ref[0, pl.ds(r0, tq), :] = _rms(o, gain).astype(o_ref.dtype)
        return carry

    lax.fori_loop(0, l // tq, body, 0, unroll=4)


def _diffattn(lam_p, q, k, v, subln_g, *, heads, head_dim, lambda_init, tq):
    b, l, w = q.shape
    vd = 2 * head_dim
    kern = functools.partial(_diffattn_kernel, tq=tq, head_dim=head_dim, lambda_init=lambda_init)
    blk = pl.BlockSpec((1, l, vd), lambda i, h: (i, 0, h))
    return pl.pallas_call(
        kern,
        grid=(b, heads),
        in_specs=[pl.BlockSpec((4, head_dim), lambda i, h: (0, 0)), blk, blk, blk,
                  pl.BlockSpec((1, vd), lambda i, h: (0, 0))],
        out_specs=blk,
        out_shape=jax.ShapeDtypeStruct((b, l, w), BF16),
        compiler_params=_cparams(("parallel", "parallel"), 48),
        name="diff_attention",
    )(lam_p, q, k, v, subln_g)


def _s5_disc_kernel(are_ref, aim_ref, ldt_ref, bre_ref, bim_ref, pwre_ref, pwim_ref, bbre_ref, bbim_ref):
    for d in range(are_ref.shape[0]):
        a_re = are_ref[d]
        a_im = aim_ref[d]
        dt = jnp.exp(ldt_ref[d])
        mag = jnp.exp(a_re * dt)
        ang = a_im * dt
        ab_re = mag * jnp.cos(ang)
        ab_im = mag * jnp.sin(ang)
        den = a_re * a_re + a_im * a_im
        nr = ab_re - 1.0
        coef_re = (nr * a_re + ab_im * a_im) / den
        coef_im = (ab_im * a_re - nr * a_im) / den
        b_re = bre_ref[d]
        b_im = bim_ref[d]
        bbre_ref[d] = coef_re[None] * b_re - coef_im[None] * b_im
        bbim_ref[d] = coef_re[None] * b_im + coef_im[None] * b_re
        p_re = jnp.ones_like(ab_re)
        p_im = jnp.zeros_like(ab_re)
        for n in range(pwre_ref.shape[1]):
            pwre_ref[d, n] = p_re
            pwim_ref[d, n] = p_im
            p_re, p_im = p_re * ab_re - p_im * ab_im, p_re * ab_im + p_im * ab_re


def _s5_discretize(a_re, a_im, log_dt, b_re, b_im, *, chunk):
    two, g, p, hg = b_re.shape
    bt_re = jnp.transpose(b_re, (0, 3, 1, 2))
    bt_im = jnp.transpose(b_im, (0, 3, 1, 2))
    return pl.pallas_call(
        _s5_disc_kernel,
        out_shape=[jax.ShapeDtypeStruct((two, chunk + 1, g, p), F32)] * 2
        + [jax.ShapeDtypeStruct((two, hg, g, p), F32)] * 2,
        name="s5_discretize",
    )(a_re.astype(F32), a_im.astype(F32), log_dt.astype(F32)[..., None], bt_re.astype(F32), bt_im.astype(F32))


def _s5_chunk_operators(pw_re, pw_im, bb_re, bb_im, c_re, c_im, *, chunk):
    hi = lax.Precision.HIGHEST
    two, hg, g, p = bb_re.shape
    pw = (pw_re + 1j * pw_im).astype(jnp.complex64)
    bb = (bb_re + 1j * bb_im).astype(jnp.complex64)
    cc = (c_re + 1j * c_im).astype(jnp.complex64)
    kern = jnp.real(jnp.einsum('dgop,dngp,dhgp->dngoh', cc, pw[:, :chunk], bb, precision=hi))
    kf, kb = kern[0], kern[1]
    lag = jnp.arange(chunk)[None, :] - jnp.arange(chunk)[:, None]
    t_f = jnp.where((lag >= 0)[:, :, None, None, None], kf[jnp.clip(lag, 0, chunk - 1)], 0.0)
    t_b = jnp.where((lag <= 0)[:, :, None, None, None], kb[jnp.clip(-lag, 0, chunk - 1)], 0.0)
    t_op = jnp.transpose(t_f + t_b, (2, 0, 4, 1, 3)).reshape(g, chunk * hg, chunk * hg)
    inj_f = jnp.einsum('sgp,hgp->gshp', pw[0, :chunk][::-1], bb[0])
    inj_b = jnp.einsum('sgp,hgp->gshp', pw[1, :chunk], bb[1])
    p_op = jnp.concatenate([jnp.real(inj_f), jnp.real(inj_b), jnp.imag(inj_f), jnp.imag(inj_b)], axis=-1)
    p_op = p_op.reshape(g, chunk * hg, 4 * p)
    out_f = jnp.einsum('gop,tgp->gpto', cc[0], pw[0, 1:chunk + 1])
    out_b = jnp.einsum('gop,tgp->gpto', cc[1], pw[1, 1:chunk + 1][::-1])
    q_op = jnp.concatenate([jnp.real(out_f), jnp.real(out_b), -jnp.imag(out_f), -jnp.imag(out_b)], axis=1)
    q_op = q_op.reshape(g, 4 * p, chunk * hg)
    a_re = jnp.concatenate([pw_re[0, chunk], pw_re[1, chunk]], axis=-1)[:, None, :]
    a_im = jnp.concatenate([pw_im[0, chunk], pw_im[1, chunk]], axis=-1)[:, None, :]
    return t_op.astype(BF16), p_op.astype(BF16), q_op.astype(BF16), a_re, a_im


def _s5_chunk_kernel(u_ref, t_ref, p_ref, q_ref, are_ref, aim_ref, y_ref, st_ref, *, nb):
    rows = u_ref.shape[1]
    nch = rows // nb
    sl = are_ref.shape[2]
    u = u_ref[0]
    st_ref[...] = _dot(u, p_ref[0])
    a_re = jnp.broadcast_to(are_ref[0], (nb, sl))
    a_im = jnp.broadcast_to(aim_ref[0], (nb, sl))
    fwd = lax.broadcasted_iota(I32, (nb, sl), 1) < sl // 2

    def step(i, carry):
        s_re, s_im = carry
        rf = pl.ds(pl.multiple_of(i * nb, nb), nb)
        rb = pl.ds(pl.multiple_of((nch - 1 - i) * nb, nb), nb)
        inj_re = jnp.where(fwd, st_ref[rf, 0:sl], st_ref[rb, 0:sl])
        inj_im = jnp.where(fwd, st_ref[rf, sl:2 * sl], st_ref[rb, sl:2 * sl])
        st_ref[rf, 0:sl // 2] = s_re[:, :sl // 2]
        st_ref[rb, sl // 2:sl] = s_re[:, sl // 2:]
        st_ref[rf, sl:sl + sl // 2] = s_im[:, :sl // 2]
        st_ref[rb, sl + sl // 2:2 * sl] = s_im[:, sl // 2:]
        return (a_re * s_re - a_im * s_im + inj_re, a_re * s_im + a_im * s_re + inj_im)

    zero = jnp.zeros((nb, sl), F32)
    lax.fori_loop(0, nch, step, (zero, zero), unroll=2)
    y_ref[0] = (_dot(u, t_ref[0]) + _dot(st_ref[...].astype(BF16), q_ref[0])).astype(y_ref.dtype)


def _s5_mixer(u, pw_re, pw_im, bb_re, bb_im, c_re, c_im, *, chunk):
    b, l, w = u.shape
    two, hg, g, p = bb_re.shape
    nch = l // chunk
    t_op, p_op, q_op, a_re, a_im = _s5_chunk_operators(pw_re, pw_im, bb_re, bb_im, c_re, c_im, chunk=chunk)
    ug = u.reshape(b, nch, chunk, g, hg).transpose(3, 1, 0, 2, 4).reshape(g, nch * b, chunk * hg).astype(BF16)
    per_group = lambda a: pl.BlockSpec((1,) + a.shape[1:], lambda i: (i,) + (0,) * (a.ndim - 1))
    yg = pl.pallas_call(
        functools.partial(_s5_chunk_kernel, nb=b),
        grid=(g,),
        in_specs=[per_group(ug), per_group(t_op), per_group(p_op), per_group(q_op), per_group(a_re), per_group(a_im)],
        out_specs=per_group(ug),
        out_shape=jax.ShapeDtypeStruct(ug.shape, BF16),
        scratch_shapes=[pltpu.VMEM((nch * b, 4 * p), F32)],
        compiler_params=_cparams(("parallel",), 32),
        name="s5_chunked_scan",
    )(ug, t_op, p_op, q_op, a_re, a_im)
    return yg.reshape(g, nch, b, chunk, hg).transpose(2, 1, 3, 0, 4).reshape(b, l, w)


def _memkv_kernel(m_ref, g_ref, w_ref, k_ref, v_ref):
    d = m_ref.shape[2]
    mn = _rms(m_ref[0], g_ref[...]).astype(BF16)
    k_ref[0] = _dot(mn, w_ref[:, :d]).astype(BF16)
    v_ref[0] = _dot(mn, w_ref[:, d:]).astype(BF16)


def _memkv(mem, g, w_xkv):
    b, n, d = mem.shape
    blk = pl.BlockSpec((1, n, d), lambda i: (i, 0, 0))
    return pl.pallas_call(
        _memkv_kernel,
        grid=(b,),
        in_specs=[blk, pl.BlockSpec((1, d), lambda i: (0, 0)), pl.BlockSpec((d, 2 * d), lambda i: (0, 0))],
        out_specs=[blk, blk],
        out_shape=[jax.ShapeDtypeStruct((b, n, d), BF16)] * 2,
        compiler_params=_cparams(("parallel",), 32),
        name="mem_kv_proj",
    )(mem, g, w_xkv)


def _mix_cross_kernel(x_ref, oa_ref, ys_ref, u_ref, ga_ref, gs_ref, dskip_ref, wglu_ref, wba_ref, wbs_ref, wmo_ref,
                      gc_ref, wxq_ref, kx_ref, vx_ref, wxo_ref, gf_ref, wrh_ref, wrl_ref, br_ref,
                      h_ref, hp_ref, lg_ref, *, heads):
    d = x_ref.shape[2]
    hd = d // heads
    sw = u_ref.shape[2]
    ys = jax.nn.gelu(ys_ref[0].astype(F32) + dskip_ref[...] * u_ref[0]).astype(BF16)
    vg = _dot(ys, wglu_ref[...])
    o_s = (vg[:, :sw] * jax.nn.sigmoid(vg[:, sw:])).astype(BF16)
    merged = (jax.nn.sigmoid(ga_ref[0].astype(F32)) * _dot(oa_ref[0], wba_ref[...])
              + jax.nn.sigmoid(gs_ref[0].astype(F32)) * _dot(o_s, wbs_ref[...]))
    h1 = x_ref[0] + _dot(merged.astype(BF16), wmo_ref[...])

    qx = (_dot(_rms(h1, gc_ref[...]).astype(BF16), wxq_ref[...]) * (hd ** -0.5 * LOG2E)).astype(BF16)
    outs = []
    for hh in range(heads):
        sl = slice(hh * hd, (hh + 1) * hd)
        s = _dot_nt(qx[:, sl], kx_ref[0, :, sl])
        p = jnp.exp2(s - jnp.max(s, axis=-1, keepdims=True))
        p = p * (1.0 / jnp.sum(p, axis=-1, keepdims=True))
        outs.append(_dot(p.astype(BF16), vx_ref[0, :, sl]))
    h2 = h1 + _dot(jnp.concatenate(outs, axis=1).astype(BF16), wxo_ref[...])
    h_ref[0] = h2

    hn = _rms(h2, gf_ref[...])
    hp_ref[...] = _pack_bf16_pairs(hn)
    hi = hn.astype(BF16)
    lo = (hn - hi.astype(F32)).astype(BF16)
    lg_ref[...] = (_dot_nt(wrh_ref[...], hi) + _dot_nt(wrh_ref[...], lo) + _dot_nt(wrl_ref[...], hi)) + br_ref[...]


def _mix_cross(x, o_a, y_s, u, g_a, g_s, d_skip, w_glu, wba, wbs, wmo, gc, wxq, kx, vx, wxo, gf, wr_hi, wr_lo, b_r,
               *, tl):
    b, l, d = x.shape
    aw = o_a.shape[2]
    sw = u.shape[2]
    n_mem = kx.shape[1]
    e = wr_hi.shape[0]
    nt = l // tl
    kern = functools.partial(_mix_cross_kernel, heads=X_HEADS)
    tok = lambda width: pl.BlockSpec((1, tl, width), lambda i, j: (i, j, 0))
    full = lambda a: pl.BlockSpec(a.shape, lambda i, j: (0,) * a.ndim)
    return pl.pallas_call(
        kern,
        grid=(b, nt),
        in_specs=[tok(d), tok(aw), tok(sw), tok(sw), tok(d), tok(d), full(d_skip), full(w_glu),
                  full(wba), full(wbs), full(wmo), full(gc), full(wxq),
                  pl.BlockSpec((1, n_mem, d), lambda i, j: (i, 0, 0)),
                  pl.BlockSpec((1, n_mem, d), lambda i, j: (i, 0, 0)),
                  full(wxo), full(gf), full(wr_hi), full(wr_lo), full(b_r)],
        out_specs=[tok(d),
                   pl.BlockSpec((tl, d // 2), lambda i, j: (i * nt + j, 0)),
                   pl.BlockSpec((e, tl), lambda i, j: (0, i * nt + j))],
        out_shape=[jax.ShapeDtypeStruct((b, l, d), F32),
                   jax.ShapeDtypeStruct((b * l, d // 2), I32),
                   jax.ShapeDtypeStruct((e, b * l), F32)],
        compiler_params=_cparams(("parallel", "parallel"), 56),
        name="mix_cross_router",
    )(x, o_a, y_s, u, g_a, g_s, d_skip, w_glu, wba, wbs, wmo, gc, wxq, kx, vx, wxo, gf, wr_hi, wr_lo, b_r)


def _routing_kernel(lg_ref, dest_ref, gate_ref, blke_ref, nused_ref, idx_s, rank_s, cnt_s, *, tr, rows, n_blk):
    e, t = lg_ref.shape
    nt = t // tr
    ie = lax.broadcasted_iota(I32, (e, tr), 0).astype(F32)
    tri = (lax.broadcasted_iota(I32, (tr, tr), 0) <= lax.broadcasted_iota(I32, (tr, tr), 1)).astype(BF16)
    cnt_s[...] = jnp.zeros_like(cnt_s)

    def phase1(i, carry):
        cols = pl.ds(pl.multiple_of(i * tr, tr), tr)
        v = lg_ref[:, cols]
        tops, hots = [], []
        for k in range(TOP_K):
            m = jnp.max(v, axis=0, keepdims=True)
            idx = jnp.min(jnp.where(v == m, ie, float(e)), axis=0, keepdims=True)
            hot = ie == idx
            v = jnp.where(hot, -jnp.inf, v)
            tops.append(m)
            hots.append(hot)
            idx_s[k:k + 1, cols] = idx
        ex = [jnp.exp(m - tops[0]) for m in tops]
        den = ex[0] + ex[1] + ex[2] + ex[3]
        for k in range(TOP_K):
            gate_ref[k:k + 1, cols] = ex[k] / den
        hot_all = hots[0] | hots[1] | hots[2] | hots[3]
        hot_f = jnp.where(hot_all, 1.0, 0.0)
        incl = _dot(hot_f.astype(BF16), tri)
        before = cnt_s[:, 0:1] + incl - hot_f
        for k in range(TOP_K):
            rank_s[k:k + 1, cols] = jnp.sum(jnp.where(hots[k], before, 0.0), axis=0, keepdims=True)
        cnt_s[...] = cnt_s[...] + jnp.sum(hot_f, axis=1, keepdims=True)
        return carry

    lax.fori_loop(0, nt, phase1, 0)

    cnt = cnt_s[...]
    nblk_e = jnp.floor((cnt + (rows - 1.0)) * (1.0 / rows))
    row = lax.broadcasted_iota(I32, cnt.shape, 0)
    incl_b = nblk_e
    s = 1
    while s < e:
        incl_b = incl_b + jnp.where(row >= s, pltpu.roll(incl_b, s, 0), 0.0)
        s *= 2
    start_rows = (incl_b - nblk_e) * float(rows)
    start_col = start_rows[:, 0:1]

    def phase2(i, carry):
        cols = pl.ds(pl.multiple_of(i * tr, tr), tr)
        for k in range(TOP_K):
            hot = ie == idx_s[k:k + 1, cols]
            base = jnp.sum(jnp.where(hot, start_col, 0.0), axis=0, keepdims=True)
            dest_ref[k:k + 1, cols] = (base + rank_s[k:k + 1, cols]).astype(I32)
        return carry

    lax.fori_loop(0, nt, phase2, 0)

    nbp = blke_ref.shape[1]
    jb = lax.broadcasted_iota(I32, (e, nbp), 1).astype(F32)
    ends = incl_b[:, 0:1]
    be = jnp.sum(jnp.where(ends <= jb, 1.0, 0.0), axis=0, keepdims=True)
    blke_ref[...] = jnp.minimum(be, e - 1.0).astype(I32)
    nused_ref[...] = jnp.max(incl_b, axis=0, keepdims=True).astype(I32)


def _routing(logits_t, *, rows, n_blk, tr):
    e, t = logits_t.shape
    nbp = -(-n_blk // LANES) * LANES
    kern = functools.partial(_routing_kernel, tr=tr, rows=rows, n_blk=n_blk)
    return pl.pallas_call(
        kern,
        out_shape=[jax.ShapeDtypeStruct((TOP_K, t), I32), jax.ShapeDtypeStruct((TOP_K, t), F32),
                   jax.ShapeDtypeStruct((1, nbp), I32), jax.ShapeDtypeStruct((1, LANES), I32)],
        scratch_shapes=[pltpu.VMEM((TOP_K, t), F32), pltpu.VMEM((TOP_K, t), F32), pltpu.VMEM((e, LANES), F32)],
        compiler_params=pltpu.CompilerParams(vmem_limit_bytes=32 * 1024 * 1024),
        name="moe_routing",
    )(logits_t)


def _sc_mesh():
    return plsc.VectorSubcoreMesh(core_axis_name="c", subcore_axis_name="s",
                                  num_cores=V7X_SC_CORES, num_subcores=V7X_SC_SUBCORES)


def _sc_worker_id():
    return lax.axis_index("s") * V7X_SC_CORES + lax.axis_index("c")


def _dispatch_sc(dest_chunks, hp, *, n_pad):
    t, c = hp.shape
    per_worker = t // SC_CHUNK // (V7X_SC_CORES * V7X_SC_SUBCORES)

    @functools.partial(
        pl.kernel, mesh=_sc_mesh(), out_type=jax.ShapeDtypeStruct((n_pad, c), I32),
        scratch_types=[pltpu.VMEM((TOP_K, SC_CHUNK), I32), pltpu.VMEM((SC_CHUNK, c), I32), pltpu.SemaphoreType.DMA],
        name="moe_dispatch_sc")
    def scatter_rows(dest_hbm, hp_hbm, xs_hbm, idx_v, rows_v, sem):
        wid = _sc_worker_id()

        @pl.loop(0, per_worker)
        def _(i):
            chunk = wid * per_worker + i
            pltpu.sync_copy(dest_hbm.at[chunk], idx_v)
            pltpu.sync_copy(hp_hbm.at[pl.ds(chunk * SC_CHUNK, SC_CHUNK)], rows_v)
            copies = [pltpu.async_copy(rows_v, xs_hbm.at[idx_v.at[k]], sem) for k in range(TOP_K)]
            for cp in copies:
                cp.wait()

    return scatter_rows(dest_chunks, hp)


def _gather_sc(dest_chunks, ys, *, t):
    c = ys.shape[1]
    per_worker = t // SC_CHUNK // (V7X_SC_CORES * V7X_SC_SUBCORES)

    @functools.partial(
        pl.kernel, mesh=_sc_mesh(), out_type=jax.ShapeDtypeStruct((TOP_K, t, c), I32),
        scratch_types=[pltpu.VMEM((TOP_K, SC_CHUNK), I32), pltpu.VMEM((SC_CHUNK, c), I32), pltpu.SemaphoreType.DMA],
        name="moe_gather_sc")
    def gather_rows(dest_hbm, ys_hbm, out_hbm, idx_v, rows_v, sem):
        wid = _sc_worker_id()

        @pl.loop(0, per_worker)
        def _(i):
            chunk = wid * per_worker + i
            pltpu.sync_copy(dest_hbm.at[chunk], idx_v)
            for k in range(TOP_K):
                pltpu.async_copy(ys_hbm.at[idx_v.at[k]], rows_v, sem).wait()
                pltpu.sync_copy(rows_v, out_hbm.at[k, pl.ds(chunk * SC_CHUNK, SC_CHUNK)])

    return gather_rows(dest_chunks, ys)


def _expert_kernel(blke_ref, nused_ref, xs_ref, w1_ref, b1_ref, w2_ref, b2_ref, ys_ref, w1b_ref, w2b_ref):
    j = pl.program_id(0)

    @pl.when((j == 0) | (blke_ref[j] != blke_ref[jnp.maximum(j - 1, 0)]))
    def _():
        w1b_ref[...] = w1_ref[0].astype(BF16)
        w2b_ref[...] = w2_ref[0].astype(BF16)

    @pl.when(j < nused_ref[0])
    def _():
        f = w2_ref.shape[1]
        x = _unpack_bf16_pairs(xs_ref[...]).astype(BF16)
        hid = _dot(x, w1b_ref[...]) + b1_ref[0]
        gate = jnp.minimum(hid[:, :f], SWIGLU_LIMIT)
        lin = jnp.clip(hid[:, f:], -SWIGLU_LIMIT, SWIGLU_LIMIT)
        act = gate * jax.nn.sigmoid(SWIGLU_ALPHA * gate) * (lin + 1.0)
        y = _dot(act.astype(BF16), w2b_ref[...]) + b2_ref[0]
        ys_ref[...] = _pack_bf16_pairs(y)

    @pl.when(pl.program_id(0) >= nused_ref[0])
    def _():
        ys_ref[...] = jnp.zeros_like(ys_ref)


def _experts(blk_e, n_used, xs, w1, b1, w2, b2, *, rows):
    n_pad, c = xs.shape
    e, d, f2 = w1.shape
    f = f2 // 2
    n_blk = n_pad // rows
    row_blk = lambda j, be, nu: (jnp.minimum(j, nu[0] - 1), 0)
    wsel = lambda j, be, nu: (be[j], 0, 0)
    grid_spec = pltpu.PrefetchScalarGridSpec(
        num_scalar_prefetch=2,
        grid=(n_blk,),
        in_specs=[pl.BlockSpec((rows, c), row_blk),
                  pl.BlockSpec((1, d, f2), wsel), pl.BlockSpec((1, 1, f2), wsel),
                  pl.BlockSpec((1, f, d), wsel), pl.BlockSpec((1, 1, d), wsel)],
        out_specs=pl.BlockSpec((rows, c), lambda j, be, nu: (j, 0)),
        scratch_shapes=[pltpu.VMEM((d, f2), BF16), pltpu.VMEM((f, d), BF16)],
    )
    return pl.pallas_call(
        _expert_kernel,
        grid_spec=grid_spec,
        out_shape=jax.ShapeDtypeStruct((n_pad, c), I32),
        compiler_params=_cparams(("arbitrary",), 60),
        name="moe_experts",
    )(blk_e, n_used, xs, w1, b1, w2, b2)


def _combine_kernel(h_ref, gate_ref, g_ref, rows_ref, o_ref, *, final_norm):
    gates = gate_ref[...]
    out = h_ref[...]
    for k in range(TOP_K):
        out = out + _unpack_bf16_pairs(rows_ref[k]) * gates[:, k:k + 1]
    if final_norm:
        out = _rms(out, g_ref[...])
    o_ref[...] = out


def _combine(h, gates_t, g_final, rows, *, tm, final_norm):
    t, d = h.shape
    c = rows.shape[2]
    kern = functools.partial(_combine_kernel, final_norm=final_norm)
    return pl.pallas_call(
        kern,
        grid=(t // tm,),
        in_specs=[pl.BlockSpec((tm, d), lambda i: (i, 0)),
                  pl.BlockSpec((tm, TOP_K), lambda i: (i, 0)),
                  pl.BlockSpec((1, d), lambda i: (0, 0)),
                  pl.BlockSpec((TOP_K, tm, c), lambda i: (0, i, 0))],
        out_specs=pl.BlockSpec((tm, d), lambda i: (i, 0)),
        out_shape=jax.ShapeDtypeStruct((t, d), F32),
        compiler_params=_cparams(("parallel",), 32),
        name="moe_combine",
    )(h, gates_t, g_final, rows)


def _pick_tile(n, pref):
    t = min(n, pref)
    while n % t:
        t //= 2
    return t


def kernel(x, mem, norm_mix_g, w_in, lambda_q1, lambda_k1, lambda_q2, lambda_k2, subln_g, ssm_a_re, ssm_a_im, ssm_log_dt, ssm_b_re, ssm_b_im, ssm_c_re, ssm_c_im, ssm_d, w_glu, w_branch_attn, w_branch_ssm, w_mix_out, norm_cross_g, norm_mem_g, w_xq, w_xkv, w_xo, norm_ffn_g, w_router, b_router, w_e1, b_e1, w_e2, b_e2, norm_final_g):
    b, l, d = x.shape
    depth = w_in.shape[0]
    head_dim = lambda_q1.shape[-1]
    da_width = w_branch_attn.shape[1]
    heads = da_width // (2 * head_dim)
    ssm_width = w_branch_ssm.shape[1]
    n_exp = w_router.shape[-1]
    t = b * l
    assert d % (2 * LANES) == 0 and da_width % LANES == 0 and ssm_width % LANES == 0 and LANES % head_dim == 0
    assert head_dim & (head_dim - 1) == 0

    tl = _pick_tile(l, 512)
    tq = _pick_tile(l, 256)
    tm = _pick_tile(t, 256)
    tr = _pick_tile(t, 512)
    rows = MOE_ROWS
    n_blk = (t * TOP_K) // rows + n_exp
    n_pad = n_blk * rows

    half = head_dim // 2
    inv = ROPE_THETA ** (-jnp.arange(half, dtype=F32) * (2.0 / head_dim))
    ang = jnp.arange(l, dtype=F32)[:, None] * inv[None, :]
    cos_t = jnp.tile(jnp.concatenate([jnp.cos(ang), jnp.cos(ang)], axis=1), (1, da_width // head_dim))
    sin_t = jnp.tile(jnp.concatenate([-jnp.sin(ang), jnp.sin(ang)], axis=1), (1, da_width // head_dim))

    row = lambda v: v.astype(F32).reshape(1, -1)
    h = x.astype(F32)
    out = None
    for layer in range(depth):
        lambda_init = 0.8 - 0.6 * math.exp(-0.3 * layer)
        q, k, v, u, g_a, g_s = _inproj(h, row(norm_mix_g[layer]), w_in[layer].astype(BF16), cos_t, sin_t,
                                        da_width=da_width, ssm_width=ssm_width, head_dim=head_dim, tl=tl)
        lam_p = jnp.stack([lambda_q1[layer], lambda_k1[layer], lambda_q2[layer], lambda_k2[layer]]).astype(F32)
        o_a = _diffattn(lam_p, q, k, v, row(subln_g[layer]), heads=heads, head_dim=head_dim,
                        lambda_init=lambda_init, tq=tq)

        pw_re, pw_im, bb_re, bb_im = _s5_discretize(ssm_a_re[layer], ssm_a_im[layer], ssm_log_dt[layer],
                                                    ssm_b_re[layer], ssm_b_im[layer], chunk=S5_CHUNK)
        y_s = _s5_mixer(u, pw_re, pw_im, bb_re, bb_im, ssm_c_re[layer].astype(F32), ssm_c_im[layer].astype(F32),
                        chunk=S5_CHUNK)

        kx, vx = _memkv(mem.astype(F32), row(norm_mem_g[layer]), w_xkv[layer].astype(BF16))
        wr_t = w_router[layer].astype(F32).T
        wr_hi = wr_t.astype(BF16)
        wr_lo = (wr_t - wr_hi.astype(F32)).astype(BF16)
        h2, hp, logits_t = _mix_cross(
            h, o_a, y_s, u, g_a, g_s, row(ssm_d[layer]), w_glu[layer].astype(BF16),
            w_branch_attn[layer].astype(BF16), w_branch_ssm[layer].astype(BF16), w_mix_out[layer].astype(BF16),
            row(norm_cross_g[layer]), w_xq[layer].astype(BF16), kx, vx, w_xo[layer].astype(BF16),
            row(norm_ffn_g[layer]), wr_hi, wr_lo, b_router[layer].astype(F32).reshape(n_exp, 1), tl=tl)

        dest, gates, blk_e, n_used = _routing(logits_t, rows=rows, n_blk=n_blk, tr=tr)
        dest_chunks = dest.reshape(TOP_K, t // SC_CHUNK, SC_CHUNK).transpose(1, 0, 2)
        xs = _dispatch_sc(dest_chunks, hp, n_pad=n_pad)
        ys = _experts(blk_e[0, :n_blk], n_used[0, :1], xs,
                      w_e1[layer].astype(F32), b_e1[layer].astype(F32)[:, None, :],
                      w_e2[layer].astype(F32), b_e2[layer].astype(F32)[:, None, :], rows=rows)
        last = layer == depth - 1
        rows_k = _gather_sc(dest_chunks, ys, t=t)
        h_flat = _combine(h2.reshape(t, d), gates.T, row(norm_final_g), rows_k, tm=tm, final_norm=last)
        h = h_flat.reshape(b, l, d)
        out = h
    return out.astype(x.dtype)
```

```python
import functools
import math

import jax
import jax.numpy as jnp
from jax import lax
from jax.experimental import pallas as pl
from jax.experimental.pallas import tpu as pltpu
from jax.experimental.pallas import tpu_sc as plsc

X_HEADS = 4
TOP_K = 4
ROPE_THETA = 10000.0
SWIGLU_LIMIT = 7.0
SWIGLU_ALPHA = 1.702
EPS = 1e-6
LOG2E = 1.4426950408889634

LANES = 128
V7X_VMEM_BYTES = 64 * 1024 * 1024
V7X_SC_CORES = 2
V7X_SC_SUBCORES = 16
SC_CHUNK = 128

MOE_ROWS = 512

F32 = jnp.float32
BF16 = jnp.bfloat16
U32 = jnp.uint32
I32 = jnp.int32


def _cparams(semantics, vmem_mb):
    return pltpu.CompilerParams(dimension_semantics=semantics, vmem_limit_bytes=vmem_mb * 1024 * 1024)


def _rms(x, g):
    return x * lax.rsqrt(jnp.mean(x * x, axis=-1, keepdims=True) + EPS) * g


def _dot(a, b):
    return jnp.dot(a, b, preferred_element_type=F32)


def _dot_nt(a, b):
    return lax.dot_general(a, b, (((1,), (1,)), ((), ())), preferred_element_type=F32)


def _pack_bf16_pairs(x):
    c = x.shape[1] // 2
    lo = lax.bitcast_convert_type(x[:, :c].astype(BF16).astype(F32), U32)
    hi = lax.bitcast_convert_type(x[:, c:].astype(BF16).astype(F32), U32)
    return lax.bitcast_convert_type((hi & jnp.uint32(0xFFFF0000)) | (lo >> jnp.uint32(16)), I32)


def _unpack_bf16_pairs(w):
    w = lax.bitcast_convert_type(w, U32)
    lo = lax.bitcast_convert_type(w << jnp.uint32(16), F32)
    hi = lax.bitcast_convert_type(w & jnp.uint32(0xFFFF0000), F32)
    return jnp.concatenate([lo, hi], axis=1)


def _inproj_kernel(x_ref, g_ref, w_ref, cos_ref, sin_ref, q_ref, k_ref, v_ref, u_ref, ga_ref, gs_ref,
                   *, da_width, ssm_width, d_model, head_dim, q_scale):
    x = x_ref[0]
    hb = _rms(x, g_ref[...]).astype(BF16)
    tl = x.shape[0]
    half = head_dim // 2
    lane = lax.broadcasted_iota(I32, (tl, LANES), 1)
    first = (lane & (head_dim - 1)) < half

    def rope(z, scale):
        outs = []
        for c in range(da_width // LANES):
            zc = z[:, c * LANES:(c + 1) * LANES]
            sw = jnp.where(first, pltpu.roll(zc, LANES - half, 1), pltpu.roll(zc, half, 1))
            r = zc * cos_ref[:, c * LANES:(c + 1) * LANES] + sw * sin_ref[:, c * LANES:(c + 1) * LANES]
            outs.append(r * scale if scale != 1.0 else r)
        return jnp.concatenate(outs, axis=1)

    o = 0
    q_ref[0] = rope(_dot(hb, w_ref[:, o:o + da_width]), q_scale).astype(BF16)
    o += da_width
    k_ref[0] = rope(_dot(hb, w_ref[:, o:o + da_width]), 1.0).astype(BF16)
    o += da_width
    v_ref[0] = _dot(hb, w_ref[:, o:o + da_width]).astype(BF16)
    o += da_width
    u_ref[...] = _dot(hb, w_ref[:, o:o + ssm_width])
    o += ssm_width
    ga_ref[0] = _dot(hb, w_ref[:, o:o + d_model]).astype(BF16)
    o += d_model
    gs_ref[0] = _dot(hb, w_ref[:, o:o + d_model]).astype(BF16)


def _inproj(x, g, w_in, cos_t, sin_t, *, da_width, ssm_width, head_dim, tl):
    b, l, d = x.shape
    in_w = w_in.shape[1]
    kern = functools.partial(_inproj_kernel, da_width=da_width, ssm_width=ssm_width, d_model=d,
                             head_dim=head_dim, q_scale=head_dim ** -0.5 * LOG2E)
    tok = lambda width: pl.BlockSpec((1, tl, width), lambda i, j: (i, j, 0))
    return pl.pallas_call(
        kern,
        grid=(b, l // tl),
        in_specs=[tok(d),
                  pl.BlockSpec((1, d), lambda i, j: (0, 0)),
                  pl.BlockSpec((d, in_w), lambda i, j: (0, 0)),
                  pl.BlockSpec((tl, da_width), lambda i, j: (j, 0)),
                  pl.BlockSpec((tl, da_width), lambda i, j: (j, 0))],
        out_specs=[tok(da_width), tok(da_width), tok(da_width),
                   pl.BlockSpec((tl, ssm_width), lambda i, j: (j, i)),
                   tok(d), tok(d)],
        out_shape=[jax.ShapeDtypeStruct((b, l, da_width), BF16)] * 3
        + [jax.ShapeDtypeStruct((l, b * ssm_width), F32)]
        + [jax.ShapeDtypeStruct((b, l, d), BF16)] * 2,
        compiler_params=_cparams(("parallel", "parallel"), 48),
        name="inproj_rope",
    )(x, g, w_in, cos_t, sin_t)


def _diffattn_kernel(lam_ref, q_ref, k_ref, v_ref, g_ref, o_ref, *, tq, head_dim, lambda_init):
    l = q_ref.shape[1]
    vd = 2 * head_dim
    k = k_ref[0]
    v = v_ref[0]
    lp = lam_ref[...]
    lam = (jnp.exp(jnp.sum(lp[0:1] * lp[1:2], axis=-1, keepdims=True))
           - jnp.exp(jnp.sum(lp[2:3] * lp[3:4], axis=-1, keepdims=True)) + lambda_init)
    lane = lax.broadcasted_iota(I32, (tq, vd), 1)
    zero = jnp.zeros((), BF16)
    gain = g_ref[...] * (1.0 - lambda_init)

    def body(i, carry):
        r0 = pl.multiple_of(i * tq, tq)
        q = q_ref[0, pl.ds(r0, tq), :]
        def component(qm):
            s = _dot_nt(qm, k)
            p = jnp.exp2(s - jnp.max(s, axis=-1, keepdims=True))
            r = 1.0 / jnp.sum(p, axis=-1, keepdims=True)
            return _dot(p.astype(BF16), v) * r

        o = component(jnp.where(lane < head_dim, q, zero)) - lam * component(jnp.where(lane >= head_dim, q, zero))
        o_ref[0, pl.ds(r0, tq), :] = _rms(o, gain).astype(o_ref.dtype)
        return carry

    lax.fori_loop(0, l // tq, body, 0, unroll=4)


def _diffattn(lam_p, q, k, v, subln_g, *, heads, head_dim, lambda_init, tq):
    b, l, w = q.shape
    vd = 2 * head_dim
    kern = functools.partial(_diffattn_kernel, tq=tq, head_dim=head_dim, lambda_init=lambda_init)
    blk = pl.BlockSpec((1, l, vd), lambda i, h: (i, 0, h))
    return pl.pallas_call(
        kern,
        grid=(b, heads),
        in_specs=[pl.BlockSpec((4, head_dim), lambda i, h: (0, 0)), blk, blk, blk,
                  pl.BlockSpec((1, vd), lambda i, h: (0, 0))],
        out_specs=blk,
        out_shape=jax.ShapeDtypeStruct((b, l, w), BF16),
        compiler_params=_cparams(("parallel", "parallel"), 48),
        name="diff_attention",
    )(lam_p, q, k, v, subln_g)


def _s5_disc_kernel(are_ref, aim_ref, ldt_ref, bre_ref, bim_ref, abre_ref, abim_ref, bbre_ref, bbim_ref):
    for d in range(are_ref.shape[0]):
        a_re = are_ref[d]
        a_im = aim_ref[d]
        dt = jnp.exp(ldt_ref[d])
        mag = jnp.exp(a_re * dt)
        ang = a_im * dt
        ab_re = mag * jnp.cos(ang)
        ab_im = mag * jnp.sin(ang)
        den = a_re * a_re + a_im * a_im
        nr = ab_re - 1.0
        coef_re = (nr * a_re + ab_im * a_im) / den
        coef_im = (ab_im * a_re - nr * a_im) / den
        abre_ref[d] = ab_re
        abim_ref[d] = ab_im
        b_re = bre_ref[d]
        b_im = bim_ref[d]
        bbre_ref[d] = coef_re[None] * b_re - coef_im[None] * b_im
        bbim_ref[d] = coef_re[None] * b_im + coef_im[None] * b_re


def _s5_discretize(a_re, a_im, log_dt, b_re, b_im):
    two, g, p, hg = b_re.shape
    bt_re = jnp.transpose(b_re, (0, 3, 1, 2))
    bt_im = jnp.transpose(b_im, (0, 3, 1, 2))
    return pl.pallas_call(
        _s5_disc_kernel,
        out_shape=[jax.ShapeDtypeStruct((two, g, p), F32)] * 2 + [jax.ShapeDtypeStruct((two, hg, g, p), F32)] * 2,
        name="s5_discretize",
    )(a_re.astype(F32), a_im.astype(F32), log_dt.astype(F32)[..., None], bt_re.astype(F32), bt_im.astype(F32))


def _block_diag(m, gb):
    g, r, c = m.shape
    mb = m.reshape(g // gb, gb, r, c)
    eye = jnp.eye(gb, dtype=m.dtype)
    return jnp.einsum('jgrc,gh->jgrhc', mb, eye).reshape(g // gb, gb * r, gb * c)


def _s5_scan_chunk(u_ref, bre_ref, bim_ref, are_ref, aim_ref, cre_ref, cim_ref, sre_ref, sim_ref, bufre, bufim,
                   *, reverse):
    tc, nb, w = u_ref.shape
    nbund = w // LANES
    lw = are_ref.shape[-1]
    ys = []
    for j in range(nbund):
        ub = u_ref[:, :, j * LANES:(j + 1) * LANES].reshape(tc * nb, LANES).astype(BF16)
        bufre[j] = _dot(ub, bre_ref[j])
        bufim[j] = _dot(ub, bim_ref[j])
        a_re = jnp.broadcast_to(are_ref[j], (nb, lw))
        a_im = jnp.broadcast_to(aim_ref[j], (nb, lw))

        xr, xi = sre_ref[j], sim_ref[j]
        for t in range(tc):
            tt = (tc - 1 - t) if reverse else t
            rows = slice(tt * nb, (tt + 1) * nb)
            xr, xi = (a_re * xr - a_im * xi + bufre[j, rows, :], a_re * xi + a_im * xr + bufim[j, rows, :])
            bufre[j, rows, :] = xr
            bufim[j, rows, :] = xi
        sre_ref[j] = xr
        sim_ref[j] = xi
        ys.append(_dot(bufre[j].astype(BF16), cre_ref[j]) - _dot(bufim[j].astype(BF16), cim_ref[j]))
    return ys


def _s5_fwd_kernel(u_ref, bre_ref, bim_ref, are_ref, aim_ref, cre_ref, cim_ref, y_ref, sre_ref, sim_ref,
                   bufre, bufim):
    @pl.when(pl.program_id(0) == 0)
    def _():
        sre_ref[...] = jnp.zeros_like(sre_ref)
        sim_ref[...] = jnp.zeros_like(sim_ref)

    tc, nb, _ = u_ref.shape
    ys = _s5_scan_chunk(u_ref, bre_ref, bim_ref, are_ref, aim_ref, cre_ref, cim_ref, sre_ref, sim_ref,
                        bufre, bufim, reverse=False)
    for j, y in enumerate(ys):
        y_ref[:, :, j * LANES:(j + 1) * LANES] = y.reshape(tc, nb, LANES)


def _s5_bwd_kernel(u_ref, yf_ref, bre_ref, bim_ref, are_ref, aim_ref, cre_ref, cim_ref, d_ref, wglu_ref, o_ref,
                   sre_ref, sim_ref, bufre, bufim):
    @pl.when(pl.program_id(0) == 0)
    def _():
        sre_ref[...] = jnp.zeros_like(sre_ref)
        sim_ref[...] = jnp.zeros_like(sim_ref)

    tc, nb, w = u_ref.shape
    ys = _s5_scan_chunk(u_ref, bre_ref, bim_ref, are_ref, aim_ref, cre_ref, cim_ref, sre_ref, sim_ref,
                        bufre, bufim, reverse=True)
    y = jnp.concatenate(ys, axis=1)
    y = y + yf_ref[...].reshape(tc * nb, w) + d_ref[...] * u_ref[...].reshape(tc * nb, w)
    y = jax.nn.gelu(y).astype(BF16)
    vg = _dot(y, wglu_ref[...])
    o = vg[:, :w] * jax.nn.sigmoid(vg[:, w:])
    o_ref[...] = o.reshape(tc, nb, w).astype(o_ref.dtype)


def _s5_mixer(u3, ab_re, ab_im, bb_re, bb_im, c_re, c_im, d_skip, w_glu, *, tc):
    l, nb, w = u3.shape
    two, hg, g, p = bb_re.shape
    gb = LANES // hg
    nbund = g // gb
    lw = gb * p
    nch = l // tc

    def direction_params(d):
        bre = _block_diag(jnp.swapaxes(bb_re[d], 0, 1), gb).astype(BF16)
        bim = _block_diag(jnp.swapaxes(bb_im[d], 0, 1), gb).astype(BF16)
        cre = _block_diag(jnp.swapaxes(c_re[d], 1, 2), gb).astype(BF16)
        cim = _block_diag(jnp.swapaxes(c_im[d], 1, 2), gb).astype(BF16)
        are = ab_re[d].reshape(nbund, 1, lw)
        aim = ab_im[d].reshape(nbund, 1, lw)
        return bre, bim, are, aim, cre, cim

    full = lambda a: pl.BlockSpec(a.shape, lambda i: (0,) * a.ndim)
    scratch = [pltpu.VMEM((nbund, nb, lw), F32), pltpu.VMEM((nbund, nb, lw), F32),
               pltpu.VMEM((nbund, tc * nb, lw), F32), pltpu.VMEM((nbund, tc * nb, lw), F32)]

    pf = direction_params(0)
    y_f = pl.pallas_call(
        _s5_fwd_kernel,
        grid=(nch,),
        in_specs=[pl.BlockSpec((tc, nb, w), lambda i: (i, 0, 0))] + [full(a) for a in pf],
        out_specs=pl.BlockSpec((tc, nb, w), lambda i: (i, 0, 0)),
        out_shape=jax.ShapeDtypeStruct((l, nb, w), F32),
        scratch_shapes=scratch,
        compiler_params=_cparams(("arbitrary",), 48),
        name="s5_forward_scan",
    )(u3, *pf)

    pb = direction_params(1)
    rev = lambda i: (nch - 1 - i, 0, 0)
    return pl.pallas_call(
        _s5_bwd_kernel,
        grid=(nch,),
        in_specs=[pl.BlockSpec((tc, nb, w), rev), pl.BlockSpec((tc, nb, w), rev)] + [full(a) for a in pb]
        + [pl.BlockSpec((1, w), lambda i: (0, 0)), pl.BlockSpec(w_glu.shape, lambda i: (0, 0))],
        out_specs=pl.BlockSpec((tc, nb, w), rev),
        out_shape=jax.ShapeDtypeStruct((l, nb, w), BF16),
        scratch_shapes=scratch,
        compiler_params=_cparams(("arbitrary",), 48),
        name="s5_backward_scan_glu",
    )(u3, y_f, *pb, d_skip, w_glu)


def _memkv_kernel(m_ref, g_ref, w_ref, k_ref, v_ref):
    d = m_ref.shape[2]
    mn = _rms(m_ref[0], g_ref[...]).astype(BF16)
    k_ref[0] = _dot(mn, w_ref[:, :d]).astype(BF16)
    v_ref[0] = _dot(mn, w_ref[:, d:]).astype(BF16)


def _memkv(mem, g, w_xkv):
    b, n, d = mem.shape
    blk = pl.BlockSpec((1, n, d), lambda i: (i, 0, 0))
    return pl.pallas_call(
        _memkv_kernel,
        grid=(b,),
        in_specs=[blk, pl.BlockSpec((1, d), lambda i: (0, 0)), pl.BlockSpec((d, 2 * d), lambda i: (0, 0))],
        out_specs=[blk, blk],
        out_shape=[jax.ShapeDtypeStruct((b, n, d), BF16)] * 2,
        compiler_params=_cparams(("parallel",), 32),
        name="mem_kv_proj",
    )(mem, g, w_xkv)


def _mix_cross_kernel(x_ref, oa_ref, os_ref, ga_ref, gs_ref, wba_ref, wbs_ref, wmo_ref, gc_ref, wxq_ref,
                      kx_ref, vx_ref, wxo_ref, gf_ref, wrh_ref, wrl_ref, br_ref,
                      h_ref, hp_ref, lg_ref, *, heads, sub):
    d = x_ref.shape[2]
    hd = d // heads
    tl = x_ref.shape[1]
    for r0 in range(0, tl, sub):
        rs = slice(r0, r0 + sub)
        merged = (jax.nn.sigmoid(ga_ref[0, rs, :].astype(F32)) * _dot(oa_ref[0, rs, :], wba_ref[...])
                  + jax.nn.sigmoid(gs_ref[0, rs, :].astype(F32)) * _dot(os_ref[rs, :], wbs_ref[...]))
        h1 = x_ref[0, rs, :] + _dot(merged.astype(BF16), wmo_ref[...])

        qx = (_dot(_rms(h1, gc_ref[...]).astype(BF16), wxq_ref[...]) * (hd ** -0.5 * LOG2E)).astype(BF16)
        outs = []
        for hh in range(heads):
            sl = slice(hh * hd, (hh + 1) * hd)
            s = _dot_nt(qx[:, sl], kx_ref[0, :, sl])
            p = jnp.exp2(s - jnp.max(s, axis=-1, keepdims=True))
            p = p * (1.0 / jnp.sum(p, axis=-1, keepdims=True))
            outs.append(_dot(p.astype(BF16), vx_ref[0, :, sl]))
        h2 = h1 + _dot(jnp.concatenate(outs, axis=1).astype(BF16), wxo_ref[...])
        h_ref[0, rs, :] = h2

        hn = _rms(h2, gf_ref[...])
        hp_ref[rs, :] = _pack_bf16_pairs(hn)
        hi = hn.astype(BF16)
        lo = (hn - hi.astype(F32)).astype(BF16)
        lg_ref[:, rs] = ((_dot_nt(wrh_ref[...], hi) + _dot_nt(wrh_ref[...], lo) + _dot_nt(wrl_ref[...], hi))
                         + br_ref[...])


def _mix_cross(x, o_a, o_s2, g_a, g_s, wba, wbs, wmo, gc, wxq, kx, vx, wxo, gf, wr_hi, wr_lo, b_r, *, tl):
    b, l, d = x.shape
    aw = o_a.shape[2]
    sw = o_s2.shape[1] // b
    n_mem = kx.shape[1]
    e = wr_hi.shape[0]
    nt = l // tl
    kern = functools.partial(_mix_cross_kernel, heads=X_HEADS, sub=tl)
    tok = lambda width: pl.BlockSpec((1, tl, width), lambda i, j: (i, j, 0))
    full = lambda a: pl.BlockSpec(a.shape, lambda i, j: (0,) * a.ndim)
    return pl.pallas_call(
        kern,
        grid=(b, nt),
        in_specs=[tok(d), tok(aw), pl.BlockSpec((tl, sw), lambda i, j: (j, i)), tok(d), tok(d),
                  full(wba), full(wbs), full(wmo), full(gc), full(wxq),
                  pl.BlockSpec((1, n_mem, d), lambda i, j: (i, 0, 0)),
                  pl.BlockSpec((1, n_mem, d), lambda i, j: (i, 0, 0)),
                  full(wxo), full(gf), full(wr_hi), full(wr_lo), full(b_r)],
        out_specs=[tok(d),
                   pl.BlockSpec((tl, d // 2), lambda i, j: (i * nt + j, 0)),
                   pl.BlockSpec((e, tl), lambda i, j: (0, i * nt + j))],
        out_shape=[jax.ShapeDtypeStruct((b, l, d), F32),
                   jax.ShapeDtypeStruct((b * l, d // 2), I32),
                   jax.ShapeDtypeStruct((e, b * l), F32)],
        compiler_params=_cparams(("parallel", "parallel"), 56),
        name="mix_cross_router",
    )(x, o_a, o_s2, g_a, g_s, wba, wbs, wmo, gc, wxq, kx, vx, wxo, gf, wr_hi, wr_lo, b_r)


def _routing_kernel(lg_ref, dest_ref, gate_ref, blke_ref, nused_ref, idx_s, rank_s, cnt_s, *, tr, rows, n_blk):
    e, t = lg_ref.shape
    nt = t // tr
    ie = lax.broadcasted_iota(I32, (e, tr), 0).astype(F32)
    tri = (lax.broadcasted_iota(I32, (tr, tr), 0) <= lax.broadcasted_iota(I32, (tr, tr), 1)).astype(BF16)
    cnt_s[...] = jnp.zeros_like(cnt_s)

    def phase1(i, carry):
        cols = pl.ds(pl.multiple_of(i * tr, tr), tr)
        v = lg_ref[:, cols]
        tops, hots = [], []
        for k in range(TOP_K):
            m = jnp.max(v, axis=0, keepdims=True)
            idx = jnp.min(jnp.where(v == m, ie, float(e)), axis=0, keepdims=True)
            hot = ie == idx
            v = jnp.where(hot, -jnp.inf, v)
            tops.append(m)
            hots.append(hot)
            idx_s[k:k + 1, cols] = idx
        ex = [jnp.exp(m - tops[0]) for m in tops]
        den = ex[0] + ex[1] + ex[2] + ex[3]
        for k in range(TOP_K):
            gate_ref[k:k + 1, cols] = ex[k] / den
        hot_all = hots[0] | hots[1] | hots[2] | hots[3]
        hot_f = jnp.where(hot_all, 1.0, 0.0)
        incl = _dot(hot_f.astype(BF16), tri)
        before = cnt_s[:, 0:1] + incl - hot_f
        for k in range(TOP_K):
            rank_s[k:k + 1, cols] = jnp.sum(jnp.where(hots[k], before, 0.0), axis=0, keepdims=True)
        cnt_s[...] = cnt_s[...] + jnp.sum(hot_f, axis=1, keepdims=True)
        return carry

    lax.fori_loop(0, nt, phase1, 0)

    cnt = cnt_s[...]
    nblk_e = jnp.floor((cnt + (rows - 1.0)) * (1.0 / rows))
    row = lax.broadcasted_iota(I32, cnt.shape, 0)
    incl_b = nblk_e
    s = 1
    while s < e:
        incl_b = incl_b + jnp.where(row >= s, pltpu.roll(incl_b, s, 0), 0.0)
        s *= 2
    start_rows = (incl_b - nblk_e) * float(rows)
    start_col = start_rows[:, 0:1]

    def phase2(i, carry):
        cols = pl.ds(pl.multiple_of(i * tr, tr), tr)
        for k in range(TOP_K):
            hot = ie == idx_s[k:k + 1, cols]
            base = jnp.sum(jnp.where(hot, start_col, 0.0), axis=0, keepdims=True)
            dest_ref[k:k + 1, cols] = (base + rank_s[k:k + 1, cols]).astype(I32)
        return carry

    lax.fori_loop(0, nt, phase2, 0)

    nbp = blke_ref.shape[1]
    jb = lax.broadcasted_iota(I32, (e, nbp), 1).astype(F32)
    ends = incl_b[:, 0:1]
    be = jnp.sum(jnp.where(ends <= jb, 1.0, 0.0), axis=0, keepdims=True)
    blke_ref[...] = jnp.minimum(be, e - 1.0).astype(I32)
    nused_ref[...] = jnp.max(incl_b, axis=0, keepdims=True).astype(I32)


def _routing(logits_t, *, rows, n_blk, tr):
    e, t = logits_t.shape
    nbp = -(-n_blk // LANES) * LANES
    kern = functools.partial(_routing_kernel, tr=tr, rows=rows, n_blk=n_blk)
    return pl.pallas_call(
        kern,
        out_shape=[jax.ShapeDtypeStruct((TOP_K, t), I32), jax.ShapeDtypeStruct((TOP_K, t), F32),
                   jax.ShapeDtypeStruct((1, nbp), I32), jax.ShapeDtypeStruct((1, LANES), I32)],
        scratch_shapes=[pltpu.VMEM((TOP_K, t), F32), pltpu.VMEM((TOP_K, t), F32), pltpu.VMEM((e, LANES), F32)],
        compiler_params=pltpu.CompilerParams(vmem_limit_bytes=32 * 1024 * 1024),
        name="moe_routing",
    )(logits_t)


def _sc_mesh():
    return plsc.VectorSubcoreMesh(core_axis_name="c", subcore_axis_name="s",
                                  num_cores=V7X_SC_CORES, num_subcores=V7X_SC_SUBCORES)


def _sc_worker_id():
    return lax.axis_index("s") * V7X_SC_CORES + lax.axis_index("c")


def _dispatch_sc(dest_chunks, hp, *, n_pad):
    t, c = hp.shape
    per_worker = t // SC_CHUNK // (V7X_SC_CORES * V7X_SC_SUBCORES)

    @functools.partial(
        pl.kernel, mesh=_sc_mesh(), out_type=jax.ShapeDtypeStruct((n_pad, c), I32),
        scratch_types=[pltpu.VMEM((TOP_K, SC_CHUNK), I32), pltpu.VMEM((SC_CHUNK, c), I32), pltpu.SemaphoreType.DMA],
        name="moe_dispatch_sc")
    def scatter_rows(dest_hbm, hp_hbm, xs_hbm, idx_v, rows_v, sem):
        wid = _sc_worker_id()

        @pl.loop(0, per_worker)
        def _(i):
            chunk = wid * per_worker + i
            pltpu.sync_copy(dest_hbm.at[chunk], idx_v)
            pltpu.sync_copy(hp_hbm.at[pl.ds(chunk * SC_CHUNK, SC_CHUNK)], rows_v)
            copies = [pltpu.async_copy(rows_v, xs_hbm.at[idx_v.at[k]], sem) for k in range(TOP_K)]
            for cp in copies:
                cp.wait()

    return scatter_rows(dest_chunks, hp)


def _gather_sc(dest_chunks, ys, *, t):
    c = ys.shape[1]
    per_worker = t // SC_CHUNK // (V7X_SC_CORES * V7X_SC_SUBCORES)

    @functools.partial(
        pl.kernel, mesh=_sc_mesh(), out_type=jax.ShapeDtypeStruct((TOP_K, t, c), I32),
        scratch_types=[pltpu.VMEM((TOP_K, SC_CHUNK), I32), pltpu.VMEM((SC_CHUNK, c), I32), pltpu.SemaphoreType.DMA],
        name="moe_gather_sc")
    def gather_rows(dest_hbm, ys_hbm, out_hbm, idx_v, rows_v, sem):
        wid = _sc_worker_id()

        @pl.loop(0, per_worker)
        def _(i):
            chunk = wid * per_worker + i
            pltpu.sync_copy(dest_hbm.at[chunk], idx_v)
            for k in range(TOP_K):
                pltpu.async_copy(ys_hbm.at[idx_v.at[k]], rows_v, sem).wait()
                pltpu.sync_copy(rows_v, out_hbm.at[k, pl.ds(chunk * SC_CHUNK, SC_CHUNK)])

    return gather_rows(dest_chunks, ys)


def _expert_kernel(blke_ref, nused_ref, xs_ref, w1_ref, b1_ref, w2_ref, b2_ref, ys_ref, w1b_ref, w2b_ref):
    j = pl.program_id(0)

    @pl.when((j == 0) | (blke_ref[j] != blke_ref[jnp.maximum(j - 1, 0)]))
    def _():
        w1b_ref[...] = w1_ref[0].astype(BF16)
        w2b_ref[...] = w2_ref[0].astype(BF16)

    @pl.when(j < nused_ref[0])
    def _():
        f = w2_ref.shape[1]
        x = _unpack_bf16_pairs(xs_ref[...]).astype(BF16)
        hid = _dot(x, w1b_ref[...]) + b1_ref[0]
        gate = jnp.minimum(hid[:, :f], SWIGLU_LIMIT)
        lin = jnp.clip(hid[:, f:], -SWIGLU_LIMIT, SWIGLU_LIMIT)
        act = gate * jax.nn.sigmoid(SWIGLU_ALPHA * gate) * (lin + 1.0)
        y = _dot(act.astype(BF16), w2b_ref[...]) + b2_ref[0]
        ys_ref[...] = _pack_bf16_pairs(y)

    @pl.when(pl.program_id(0) >= nused_ref[0])
    def _():
        ys_ref[...] = jnp.zeros_like(ys_ref)


def _experts(blk_e, n_used, xs, w1, b1, w2, b2, *, rows):
    n_pad, c = xs.shape
    e, d, f2 = w1.shape
    f = f2 // 2
    n_blk = n_pad // rows
    row_blk = lambda j, be, nu: (jnp.minimum(j, nu[0] - 1), 0)
    wsel = lambda j, be, nu: (be[j], 0, 0)
    grid_spec = pltpu.PrefetchScalarGridSpec(
        num_scalar_prefetch=2,
        grid=(n_blk,),
        in_specs=[pl.BlockSpec((rows, c), row_blk),
                  pl.BlockSpec((1, d, f2), wsel), pl.BlockSpec((1, 1, f2), wsel),
                  pl.BlockSpec((1, f, d), wsel), pl.BlockSpec((1, 1, d), wsel)],
        out_specs=pl.BlockSpec((rows, c), lambda j, be, nu: (j, 0)),
        scratch_shapes=[pltpu.VMEM((d, f2), BF16), pltpu.VMEM((f, d), BF16)],
    )
    return pl.pallas_call(
        _expert_kernel,
        grid_spec=grid_spec,
        out_shape=jax.ShapeDtypeStruct((n_pad, c), I32),
        compiler_params=_cparams(("arbitrary",), 60),
        name="moe_experts",
    )(blk_e, n_used, xs, w1, b1, w2, b2)


def _combine_kernel(h_ref, gate_ref, g_ref, rows_ref, o_ref, *, final_norm):
    gates = gate_ref[...]
    out = h_ref[...]
    for k in range(TOP_K):
        out = out + _unpack_bf16_pairs(rows_ref[k]) * gates[:, k:k + 1]
    if final_norm:
        out = _rms(out, g_ref[...])
    o_ref[...] = out


def _combine(h, gates_t, g_final, rows, *, tm, final_norm):
    t, d = h.shape
    c = rows.shape[2]
    kern = functools.partial(_combine_kernel, final_norm=final_norm)
    return pl.pallas_call(
        kern,
        grid=(t // tm,),
        in_specs=[pl.BlockSpec((tm, d), lambda i: (i, 0)),
                  pl.BlockSpec((tm, TOP_K), lambda i: (i, 0)),
                  pl.BlockSpec((1, d), lambda i: (0, 0)),
                  pl.BlockSpec((TOP_K, tm, c), lambda i: (0, i, 0))],
        out_specs=pl.BlockSpec((tm, d), lambda i: (i, 0)),
        out_shape=jax.ShapeDtypeStruct((t, d), F32),
        compiler_params=_cparams(("parallel",), 32),
        name="moe_combine",
    )(h, gates_t, g_final, rows)


def _pick_tile(n, pref):
    t = min(n, pref)
    while n % t:
        t //= 2
    return t


def kernel(x, mem, norm_mix_g, w_in, lambda_q1, lambda_k1, lambda_q2, lambda_k2, subln_g, ssm_a_re, ssm_a_im, ssm_log_dt, ssm_b_re, ssm_b_im, ssm_c_re, ssm_c_im, ssm_d, w_glu, w_branch_attn, w_branch_ssm, w_mix_out, norm_cross_g, norm_mem_g, w_xq, w_xkv, w_xo, norm_ffn_g, w_router, b_router, w_e1, b_e1, w_e2, b_e2, norm_final_g):
    b, l, d = x.shape
    depth = w_in.shape[0]
    head_dim = lambda_q1.shape[-1]
    da_width = w_branch_attn.shape[1]
    heads = da_width // (2 * head_dim)
    ssm_width = w_branch_ssm.shape[1]
    n_exp = w_router.shape[-1]
    t = b * l
    assert d % (2 * LANES) == 0 and da_width % LANES == 0 and ssm_width % LANES == 0 and LANES % head_dim == 0
    assert head_dim & (head_dim - 1) == 0

    tl = _pick_tile(l, 512)
    tq = _pick_tile(l, 256)
    tc = _pick_tile(l, 64)
    tm = _pick_tile(t, 256)
    tr = _pick_tile(t, 512)
    rows = MOE_ROWS
    n_blk = (t * TOP_K) // rows + n_exp
    n_pad = n_blk * rows

    half = head_dim // 2
    inv = ROPE_THETA ** (-jnp.arange(half, dtype=F32) * (2.0 / head_dim))
    ang = jnp.arange(l, dtype=F32)[:, None] * inv[None, :]
    cos_t = jnp.tile(jnp.concatenate([jnp.cos(ang), jnp.cos(ang)], axis=1), (1, da_width // head_dim))
    sin_t = jnp.tile(jnp.concatenate([-jnp.sin(ang), jnp.sin(ang)], axis=1), (1, da_width // head_dim))

    row = lambda v: v.astype(F32).reshape(1, -1)
    h = x.astype(F32)
    out = None
    for layer in range(depth):
        lambda_init = 0.8 - 0.6 * math.exp(-0.3 * layer)
        q, k, v, u2, g_a, g_s = _inproj(h, row(norm_mix_g[layer]), w_in[layer].astype(BF16), cos_t, sin_t,
                                        da_width=da_width, ssm_width=ssm_width, head_dim=head_dim, tl=tl)
        lam_p = jnp.stack([lambda_q1[layer], lambda_k1[layer], lambda_q2[layer], lambda_k2[layer]]).astype(F32)
        o_a = _diffattn(lam_p, q, k, v, row(subln_g[layer]), heads=heads, head_dim=head_dim,
                        lambda_init=lambda_init, tq=tq)

        ab_re, ab_im, bb_re, bb_im = _s5_discretize(ssm_a_re[layer], ssm_a_im[layer], ssm_log_dt[layer],
                                                    ssm_b_re[layer], ssm_b_im[layer])
        o_s = _s5_mixer(u2.reshape(l, b, ssm_width), ab_re, ab_im, bb_re, bb_im,
                        ssm_c_re[layer].astype(F32), ssm_c_im[layer].astype(F32),
                        row(ssm_d[layer]), w_glu[layer].astype(BF16), tc=tc)

        kx, vx = _memkv(mem.astype(F32), row(norm_mem_g[layer]), w_xkv[layer].astype(BF16))
        wr_t = w_router[layer].astype(F32).T
        wr_hi = wr_t.astype(BF16)
        wr_lo = (wr_t - wr_hi.astype(F32)).astype(BF16)
        h2, hp, logits_t = _mix_cross(
            h, o_a, o_s.reshape(l, b * ssm_width), g_a, g_s,
            w_branch_attn[layer].astype(BF16), w_branch_ssm[layer].astype(BF16), w_mix_out[layer].astype(BF16),
            row(norm_cross_g[layer]), w_xq[layer].astype(BF16), kx, vx, w_xo[layer].astype(BF16),
            row(norm_ffn_g[layer]), wr_hi, wr_lo, b_router[layer].astype(F32).reshape(n_exp, 1), tl=tl)

        dest, gates, blk_e, n_used = _routing(logits_t, rows=rows, n_blk=n_blk, tr=tr)
        dest_chunks = dest.reshape(TOP_K, t // SC_CHUNK, SC_CHUNK).transpose(1, 0, 2)
        xs = _dispatch_sc(dest_chunks, hp, n_pad=n_pad)
        ys = _experts(blk_e[0, :n_blk], n_used[0, :1], xs,
                      w_e1[layer].astype(F32), b_e1[layer].astype(F32)[:, None, :],
                      w_e2[layer].astype(F32), b_e2[layer].astype(F32)[:, None, :], rows=rows)
        last = layer == depth - 1
        rows_k = _gather_sc(dest_chunks, ys, t=t)
        h_flat = _combine(h2.reshape(t, d), gates.T, row(norm_final_g), rows_k, tm=tm, final_norm=last)
        h = h_flat.reshape(b, l, d)
        out = h
    return out.astype(x.dtype)
```

```python
import functools
import math

import jax
import jax.numpy as jnp
from jax import lax
from jax.experimental import pallas as pl
from jax.experimental.pallas import tpu as pltpu
from jax.experimental.pallas import tpu_sc as plsc

X_HEADS = 4
TOP_K = 4
ROPE_THETA = 10000.0
SWIGLU_LIMIT = 7.0
SWIGLU_ALPHA = 1.702
EPS = 1e-6
LOG2E = 1.4426950408889634

LANES = 128
V7X_VMEM_BYTES = 64 * 1024 * 1024
V7X_SC_CORES = 2
V7X_SC_SUBCORES = 16
SC_CHUNK = 128

MOE_ROWS = 512

F32 = jnp.float32
BF16 = jnp.bfloat16
U32 = jnp.uint32
I32 = jnp.int32


def _cparams(semantics, vmem_mb):
    return pltpu.CompilerParams(dimension_semantics=semantics, vmem_limit_bytes=vmem_mb * 1024 * 1024)


def _rms(x, g):
    return x * lax.rsqrt(jnp.mean(x * x, axis=-1, keepdims=True) + EPS) * g


def _dot(a, b):
    return jnp.dot(a, b, preferred_element_type=F32)


def _dot_nt(a, b):
    return lax.dot_general(a, b, (((1,), (1,)), ((), ())), preferred_element_type=F32)


def _pack_bf16_pairs(x):
    c = x.shape[1] // 2
    lo = lax.bitcast_convert_type(x[:, :c].astype(BF16).astype(F32), U32)
    hi = lax.bitcast_convert_type(x[:, c:].astype(BF16).astype(F32), U32)
    return lax.bitcast_convert_type((hi & jnp.uint32(0xFFFF0000)) | (lo >> jnp.uint32(16)), I32)


def _unpack_bf16_pairs(w):
    w = lax.bitcast_convert_type(w, U32)
    lo = lax.bitcast_convert_type(w << jnp.uint32(16), F32)
    hi = lax.bitcast_convert_type(w & jnp.uint32(0xFFFF0000), F32)
    return jnp.concatenate([lo, hi], axis=1)


def _inproj_kernel(x_ref, g_ref, w_ref, wvt_ref, cos_ref, sin_ref, q_ref, k_ref, vt_ref, u_ref, ga_ref, gs_ref,
                   *, da_width, ssm_width, d_model, head_dim, q_scale):
    x = x_ref[0]
    hb = _rms(x, g_ref[...]).astype(BF16)
    tl = x.shape[0]
    half = head_dim // 2
    lane = lax.broadcasted_iota(I32, (tl, LANES), 1)
    first = (lane & (head_dim - 1)) < half

    def rope(z, scale):
        outs = []
        for c in range(da_width // LANES):
            zc = z[:, c * LANES:(c + 1) * LANES]
            sw = jnp.where(first, pltpu.roll(zc, LANES - half, 1), pltpu.roll(zc, half, 1))
            r = zc * cos_ref[:, c * LANES:(c + 1) * LANES] + sw * sin_ref[:, c * LANES:(c + 1) * LANES]
            outs.append(r * scale if scale != 1.0 else r)
        return jnp.concatenate(outs, axis=1)

    o = 0
    q_ref[0] = rope(_dot(hb, w_ref[:, o:o + da_width]), q_scale).astype(BF16)
    o += da_width
    k_ref[0] = rope(_dot(hb, w_ref[:, o:o + da_width]), 1.0).astype(BF16)
    o += da_width
    vt_ref[0] = _dot_nt(wvt_ref[...], hb).astype(BF16)
    o += da_width
    u_ref[...] = _dot(hb, w_ref[:, o:o + ssm_width])
    o += ssm_width
    ga_ref[0] = _dot(hb, w_ref[:, o:o + d_model]).astype(BF16)
    o += d_model
    gs_ref[0] = _dot(hb, w_ref[:, o:o + d_model]).astype(BF16)


def _inproj(x, g, w_in, w_vt, cos_t, sin_t, *, da_width, ssm_width, head_dim, tl):
    b, l, d = x.shape
    in_w = w_in.shape[1]
    kern = functools.partial(_inproj_kernel, da_width=da_width, ssm_width=ssm_width, d_model=d,
                             head_dim=head_dim, q_scale=head_dim ** -0.5 * LOG2E)
    tok = lambda width: pl.BlockSpec((1, tl, width), lambda i, j: (i, j, 0))
    return pl.pallas_call(
        kern,
        grid=(b, l // tl),
        in_specs=[tok(d),
                  pl.BlockSpec((1, d), lambda i, j: (0, 0)),
                  pl.BlockSpec((d, in_w), lambda i, j: (0, 0)),
                  pl.BlockSpec((da_width, d), lambda i, j: (0, 0)),
                  pl.BlockSpec((tl, da_width), lambda i, j: (j, 0)),
                  pl.BlockSpec((tl, da_width), lambda i, j: (j, 0))],
        out_specs=[tok(da_width), tok(da_width),
                   pl.BlockSpec((1, da_width, tl), lambda i, j: (i, 0, j)),
                   pl.BlockSpec((tl, ssm_width), lambda i, j: (j, i)),
                   tok(d), tok(d)],
        out_shape=[jax.ShapeDtypeStruct((b, l, da_width), BF16)] * 2
        + [jax.ShapeDtypeStruct((b, da_width, l), BF16)]
        + [jax.ShapeDtypeStruct((l, b * ssm_width), F32)]
        + [jax.ShapeDtypeStruct((b, l, d), BF16)] * 2,
        compiler_params=_cparams(("parallel", "parallel"), 48),
        name="inproj_rope",
    )(x, g, w_in, w_vt, cos_t, sin_t)


def _diffattn_kernel(lam_ref, q_ref, k_ref, vt_ref, g_ref, o_ref, *, tq, head_dim, lambda_init):
    l = q_ref.shape[1]
    vd = 2 * head_dim
    k = k_ref[0]
    vt = vt_ref[0]
    lp = lam_ref[...]
    lam = (jnp.exp(jnp.sum(lp[0:1] * lp[1:2], axis=-1, keepdims=True))
           - jnp.exp(jnp.sum(lp[2:3] * lp[3:4], axis=-1, keepdims=True)) + lambda_init)
    lane = lax.broadcasted_iota(I32, (tq, vd), 1)
    zero = jnp.zeros((), BF16)
    gain = g_ref[...] * (1.0 - lambda_init)

    def body(i, carry):
        r0 = pl.multiple_of(i * tq, tq)
        q = q_ref[0, pl.ds(r0, tq), :]
        def component(qm):
            st = _dot_nt(k, qm)
            p = jnp.exp2(st - jnp.max(st, axis=0, keepdims=True))
            r = 1.0 / jnp.sum(p, axis=0, keepdims=True)
            return _dot(vt, p.astype(BF16)) * r

        ot = component(jnp.where(lane < head_dim, q, zero)) - lam * component(jnp.where(lane >= head_dim, q, zero))
        o_ref[0, pl.ds(r0, tq), :] = _rms(ot.T, gain).astype(o_ref.dtype)
        return carry

    lax.fori_loop(0, l // tq, body, 0, unroll=4)


def _diffattn(lam_p, q, k, vt, subln_g, *, heads, head_dim, lambda_init, tq):
    b, l, w = q.shape
    vd = 2 * head_dim
    kern = functools.partial(_diffattn_kernel, tq=tq, head_dim=head_dim, lambda_init=lambda_init)
    blk = pl.BlockSpec((1, l, vd), lambda i, h: (i, 0, h))
    return pl.pallas_call(
        kern,
        grid=(b, heads),
        in_specs=[pl.BlockSpec((4, head_dim), lambda i, h: (0, 0)), blk, blk,
                  pl.BlockSpec((1, vd, l), lambda i, h: (i, h, 0)),
                  pl.BlockSpec((1, vd), lambda i, h: (0, 0))],
        out_specs=blk,
        out_shape=jax.ShapeDtypeStruct((b, l, w), BF16),
        compiler_params=_cparams(("parallel", "parallel"), 48),
        name="diff_attention",
    )(lam_p, q, k, vt, subln_g)


def _s5_disc_kernel(are_ref, aim_ref, ldt_ref, bre_ref, bim_ref, abre_ref, abim_ref, bbre_ref, bbim_ref):
    for d in range(are_ref.shape[0]):
        a_re = are_ref[d]
        a_im = aim_ref[d]
        dt = jnp.exp(ldt_ref[d])
        mag = jnp.exp(a_re * dt)
        ang = a_im * dt
        ab_re = mag * jnp.cos(ang)
        ab_im = mag * jnp.sin(ang)
        den = a_re * a_re + a_im * a_im
        nr = ab_re - 1.0
        coef_re = (nr * a_re + ab_im * a_im) / den
        coef_im = (ab_im * a_re - nr * a_im) / den
        abre_ref[d] = ab_re
        abim_ref[d] = ab_im
        b_re = bre_ref[d]
        b_im = bim_ref[d]
        bbre_ref[d] = coef_re[None] * b_re - coef_im[None] * b_im
        bbim_ref[d] = coef_re[None] * b_im + coef_im[None] * b_re


def _s5_discretize(a_re, a_im, log_dt, b_re, b_im):
    two, g, p, hg = b_re.shape
    bt_re = jnp.transpose(b_re, (0, 3, 1, 2))
    bt_im = jnp.transpose(b_im, (0, 3, 1, 2))
    return pl.pallas_call(
        _s5_disc_kernel,
        out_shape=[jax.ShapeDtypeStruct((two, g, p), F32)] * 2 + [jax.ShapeDtypeStruct((two, hg, g, p), F32)] * 2,
        name="s5_discretize",
    )(a_re.astype(F32), a_im.astype(F32), log_dt.astype(F32)[..., None], bt_re.astype(F32), bt_im.astype(F32))


def _block_diag(m, gb):
    g, r, c = m.shape
    mb = m.reshape(g // gb, gb, r, c)
    eye = jnp.eye(gb, dtype=m.dtype)
    return jnp.einsum('jgrc,gh->jgrhc', mb, eye).reshape(g // gb, gb * r, gb * c)


def _s5_scan_chunk(u_ref, bre_ref, bim_ref, are_ref, aim_ref, cre_ref, cim_ref, sre_ref, sim_ref, bufre, bufim,
                   *, reverse):
    tc, nb, w = u_ref.shape
    nbund = w // LANES
    lw = are_ref.shape[-1]
    ys = []
    for j in range(nbund):
        ub = u_ref[:, :, j * LANES:(j + 1) * LANES].reshape(tc * nb, LANES).astype(BF16)
        bufre[j] = _dot(ub, bre_ref[j])
        bufim[j] = _dot(ub, bim_ref[j])
        a_re = jnp.broadcast_to(are_ref[j], (nb, lw))
        a_im = jnp.broadcast_to(aim_ref[j], (nb, lw))

        xr, xi = sre_ref[j], sim_ref[j]
        for t in range(tc):
            tt = (tc - 1 - t) if reverse else t
            rows = slice(tt * nb, (tt + 1) * nb)
            xr, xi = (a_re * xr - a_im * xi + bufre[j, rows, :], a_re * xi + a_im * xr + bufim[j, rows, :])
            bufre[j, rows, :] = xr
            bufim[j, rows, :] = xi
        sre_ref[j] = xr
        sim_ref[j] = xi
        ys.append(_dot(bufre[j].astype(BF16), cre_ref[j]) - _dot(bufim[j].astype(BF16), cim_ref[j]))
    return ys


def _s5_fwd_kernel(u_ref, bre_ref, bim_ref, are_ref, aim_ref, cre_ref, cim_ref, y_ref, sre_ref, sim_ref,
                   bufre, bufim):
    @pl.when(pl.program_id(0) == 0)
    def _():
        sre_ref[...] = jnp.zeros_like(sre_ref)
        sim_ref[...] = jnp.zeros_like(sim_ref)

    tc, nb, _ = u_ref.shape
    ys = _s5_scan_chunk(u_ref, bre_ref, bim_ref, are_ref, aim_ref, cre_ref, cim_ref, sre_ref, sim_ref,
                        bufre, bufim, reverse=False)
    for j, y in enumerate(ys):
        y_ref[:, :, j * LANES:(j + 1) * LANES] = y.reshape(tc, nb, LANES)


def _s5_bwd_kernel(u_ref, yf_ref, bre_ref, bim_ref, are_ref, aim_ref, cre_ref, cim_ref, d_ref, wglu_ref, o_ref,
                   sre_ref, sim_ref, bufre, bufim):
    @pl.when(pl.program_id(0) == 0)
    def _():
        sre_ref[...] = jnp.zeros_like(sre_ref)
        sim_ref[...] = jnp.zeros_like(sim_ref)

    tc, nb, w = u_ref.shape
    ys = _s5_scan_chunk(u_ref, bre_ref, bim_ref, are_ref, aim_ref, cre_ref, cim_ref, sre_ref, sim_ref,
                        bufre, bufim, reverse=True)
    y = jnp.concatenate(ys, axis=1)
    y = y + yf_ref[...].reshape(tc * nb, w) + d_ref[...] * u_ref[...].reshape(tc * nb, w)
    y = jax.nn.gelu(y).astype(BF16)
    vg = _dot(y, wglu_ref[...])
    o = vg[:, :w] * jax.nn.sigmoid(vg[:, w:])
    o_ref[...] = o.reshape(tc, nb, w).astype(o_ref.dtype)


def _s5_mixer(u3, ab_re, ab_im, bb_re, bb_im, c_re, c_im, d_skip, w_glu, *, tc):
    l, nb, w = u3.shape
    two, hg, g, p = bb_re.shape
    gb = LANES // hg
    nbund = g // gb
    lw = gb * p
    nch = l // tc

    def direction_params(d):
        bre = _block_diag(jnp.swapaxes(bb_re[d], 0, 1), gb).astype(BF16)
        bim = _block_diag(jnp.swapaxes(bb_im[d], 0, 1), gb).astype(BF16)
        cre = _block_diag(jnp.swapaxes(c_re[d], 1, 2), gb).astype(BF16)
        cim = _block_diag(jnp.swapaxes(c_im[d], 1, 2), gb).astype(BF16)
        are = ab_re[d].reshape(nbund, 1, lw)
        aim = ab_im[d].reshape(nbund, 1, lw)
        return bre, bim, are, aim, cre, cim

    full = lambda a: pl.BlockSpec(a.shape, lambda i: (0,) * a.ndim)
    scratch = [pltpu.VMEM((nbund, nb, lw), F32), pltpu.VMEM((nbund, nb, lw), F32),
               pltpu.VMEM((nbund, tc * nb, lw), F32), pltpu.VMEM((nbund, tc * nb, lw), F32)]

    pf = direction_params(0)
    y_f = pl.pallas_call(
        _s5_fwd_kernel,
        grid=(nch,),
        in_specs=[pl.BlockSpec((tc, nb, w), lambda i: (i, 0, 0))] + [full(a) for a in pf],
        out_specs=pl.BlockSpec((tc, nb, w), lambda i: (i, 0, 0)),
        out_shape=jax.ShapeDtypeStruct((l, nb, w), F32),
        scratch_shapes=scratch,
        compiler_params=_cparams(("arbitrary",), 48),
        name="s5_forward_scan",
    )(u3, *pf)

    pb = direction_params(1)
    rev = lambda i: (nch - 1 - i, 0, 0)
    return pl.pallas_call(
        _s5_bwd_kernel,
        grid=(nch,),
        in_specs=[pl.BlockSpec((tc, nb, w), rev), pl.BlockSpec((tc, nb, w), rev)] + [full(a) for a in pb]
        + [pl.BlockSpec((1, w), lambda i: (0, 0)), pl.BlockSpec(w_glu.shape, lambda i: (0, 0))],
        out_specs=pl.BlockSpec((tc, nb, w), rev),
        out_shape=jax.ShapeDtypeStruct((l, nb, w), BF16),
        scratch_shapes=scratch,
        compiler_params=_cparams(("arbitrary",), 48),
        name="s5_backward_scan_glu",
    )(u3, y_f, *pb, d_skip, w_glu)


def _memkv_kernel(m_ref, g_ref, w_ref, k_ref, v_ref):
    d = m_ref.shape[2]
    mn = _rms(m_ref[0], g_ref[...]).astype(BF16)
    k_ref[0] = _dot(mn, w_ref[:, :d]).astype(BF16)
    v_ref[0] = _dot(mn, w_ref[:, d:]).astype(BF16)


def _memkv(mem, g, w_xkv):
    b, n, d = mem.shape
    blk = pl.BlockSpec((1, n, d), lambda i: (i, 0, 0))
    return pl.pallas_call(
        _memkv_kernel,
        grid=(b,),
        in_specs=[blk, pl.BlockSpec((1, d), lambda i: (0, 0)), pl.BlockSpec((d, 2 * d), lambda i: (0, 0))],
        out_specs=[blk, blk],
        out_shape=[jax.ShapeDtypeStruct((b, n, d), BF16)] * 2,
        compiler_params=_cparams(("parallel",), 32),
        name="mem_kv_proj",
    )(mem, g, w_xkv)


def _mix_cross_kernel(x_ref, oa_ref, os_ref, ga_ref, gs_ref, wba_ref, wbs_ref, wmo_ref, gc_ref, wxq_ref,
                      kx_ref, vx_ref, wxo_ref, gf_ref, wrh_ref, wrl_ref, br_ref,
                      h_ref, hp_ref, lg_ref, *, heads, sub):
    d = x_ref.shape[2]
    hd = d // heads
    tl = x_ref.shape[1]
    for r0 in range(0, tl, sub):
        rs = slice(r0, r0 + sub)
        merged = (jax.nn.sigmoid(ga_ref[0, rs, :].astype(F32)) * _dot(oa_ref[0, rs, :], wba_ref[...])
                  + jax.nn.sigmoid(gs_ref[0, rs, :].astype(F32)) * _dot(os_ref[rs, :], wbs_ref[...]))
        h1 = x_ref[0, rs, :] + _dot(merged.astype(BF16), wmo_ref[...])

        qx = (_dot(_rms(h1, gc_ref[...]).astype(BF16), wxq_ref[...]) * (hd ** -0.5 * LOG2E)).astype(BF16)
        outs = []
        for hh in range(heads):
            sl = slice(hh * hd, (hh + 1) * hd)
            s = _dot_nt(qx[:, sl], kx_ref[0, :, sl])
            p = jnp.exp2(s - jnp.max(s, axis=-1, keepdims=True))
            p = p * (1.0 / jnp.sum(p, axis=-1, keepdims=True))
            outs.append(_dot(p.astype(BF16), vx_ref[0, :, sl]))
        h2 = h1 + _dot(jnp.concatenate(outs, axis=1).astype(BF16), wxo_ref[...])
        h_ref[0, rs, :] = h2

        hn = _rms(h2, gf_ref[...])
        hp_ref[rs, :] = _pack_bf16_pairs(hn)
        hi = hn.astype(BF16)
        lo = (hn - hi.astype(F32)).astype(BF16)
        lg_ref[:, rs] = ((_dot_nt(wrh_ref[...], hi) + _dot_nt(wrh_ref[...], lo) + _dot_nt(wrl_ref[...], hi))
                         + br_ref[...])


def _mix_cross(x, o_a, o_s2, g_a, g_s, wba, wbs, wmo, gc, wxq, kx, vx, wxo, gf, wr_hi, wr_lo, b_r, *, tl):
    b, l, d = x.shape
    aw = o_a.shape[2]
    sw = o_s2.shape[1] // b
    n_mem = kx.shape[1]
    e = wr_hi.shape[0]
    nt = l // tl
    kern = functools.partial(_mix_cross_kernel, heads=X_HEADS, sub=tl)
    tok = lambda width: pl.BlockSpec((1, tl, width), lambda i, j: (i, j, 0))
    full = lambda a: pl.BlockSpec(a.shape, lambda i, j: (0,) * a.ndim)
    return pl.pallas_call(
        kern,
        grid=(b, nt),
        in_specs=[tok(d), tok(aw), pl.BlockSpec((tl, sw), lambda i, j: (j, i)), tok(d), tok(d),
                  full(wba), full(wbs), full(wmo), full(gc), full(wxq),
                  pl.BlockSpec((1, n_mem, d), lambda i, j: (i, 0, 0)),
                  pl.BlockSpec((1, n_mem, d), lambda i, j: (i, 0, 0)),
                  full(wxo), full(gf), full(wr_hi), full(wr_lo), full(b_r)],
        out_specs=[tok(d),
                   pl.BlockSpec((tl, d // 2), lambda i, j: (i * nt + j, 0)),
                   pl.BlockSpec((e, tl), lambda i, j: (0, i * nt + j))],
        out_shape=[jax.ShapeDtypeStruct((b, l, d), F32),
                   jax.ShapeDtypeStruct((b * l, d // 2), I32),
                   jax.ShapeDtypeStruct((e, b * l), F32)],
        compiler_params=_cparams(("parallel", "parallel"), 56),
        name="mix_cross_router",
    )(x, o_a, o_s2, g_a, g_s, wba, wbs, wmo, gc, wxq, kx, vx, wxo, gf, wr_hi, wr_lo, b_r)


def _routing_kernel(lg_ref, dest_ref, gate_ref, blke_ref, nused_ref, idx_s, rank_s, cnt_s, *, tr, rows, n_blk):
    e, t = lg_ref.shape
    nt = t // tr
    ie = lax.broadcasted_iota(I32, (e, tr), 0).astype(F32)
    tri = (lax.broadcasted_iota(I32, (tr, tr), 0) <= lax.broadcasted_iota(I32, (tr, tr), 1)).astype(BF16)
    cnt_s[...] = jnp.zeros_like(cnt_s)

    def phase1(i, carry):
        cols = pl.ds(pl.multiple_of(i * tr, tr), tr)
        v = lg_ref[:, cols]
        tops, hots = [], []
        for k in range(TOP_K):
            m = jnp.max(v, axis=0, keepdims=True)
            idx = jnp.min(jnp.where(v == m, ie, float(e)), axis=0, keepdims=True)
            hot = ie == idx
            v = jnp.where(hot, -jnp.inf, v)
            tops.append(m)
            hots.append(hot)
            idx_s[k:k + 1, cols] = idx
        ex = [jnp.exp(m - tops[0]) for m in tops]
        den = ex[0] + ex[1] + ex[2] + ex[3]
        for k in range(TOP_K):
            gate_ref[k:k + 1, cols] = ex[k] / den
        hot_all = hots[0] | hots[1] | hots[2] | hots[3]
        hot_f = jnp.where(hot_all, 1.0, 0.0)
        incl = _dot(hot_f.astype(BF16), tri)
        before = cnt_s[:, 0:1] + incl - hot_f
        for k in range(TOP_K):
            rank_s[k:k + 1, cols] = jnp.sum(jnp.where(hots[k], before, 0.0), axis=0, keepdims=True)
        cnt_s[...] = cnt_s[...] + jnp.sum(hot_f, axis=1, keepdims=True)
        return carry

    lax.fori_loop(0, nt, phase1, 0)

    cnt = cnt_s[...]
    nblk_e = jnp.floor((cnt + (rows - 1.0)) * (1.0 / rows))
    row = lax.broadcasted_iota(I32, cnt.shape, 0)
    incl_b = nblk_e
    s = 1
    while s < e:
        incl_b = incl_b + jnp.where(row >= s, pltpu.roll(incl_b, s, 0), 0.0)
        s *= 2
    start_rows = (incl_b - nblk_e) * float(rows)
    start_col = start_rows[:, 0:1]

    def phase2(i, carry):
        cols = pl.ds(pl.multiple_of(i * tr, tr), tr)
        for k in range(TOP_K):
            hot = ie == idx_s[k:k + 1, cols]
            base = jnp.sum(jnp.where(hot, start_col, 0.0), axis=0, keepdims=True)
            dest_ref[k:k + 1, cols] = (base + rank_s[k:k + 1, cols]).astype(I32)
        return carry

    lax.fori_loop(0, nt, phase2, 0)

    nbp = blke_ref.shape[1]
    jb = lax.broadcasted_iota(I32, (e, nbp), 1).astype(F32)
    ends = incl_b[:, 0:1]
    be = jnp.sum(jnp.where(ends <= jb, 1.0, 0.0), axis=0, keepdims=True)
    blke_ref[...] = jnp.minimum(be, e - 1.0).astype(I32)
    nused_ref[...] = jnp.max(incl_b, axis=0, keepdims=True).astype(I32)


def _routing(logits_t, *, rows, n_blk, tr):
    e, t = logits_t.shape
    nbp = -(-n_blk // LANES) * LANES
    kern = functools.partial(_routing_kernel, tr=tr, rows=rows, n_blk=n_blk)
    return pl.pallas_call(
        kern,
        out_shape=[jax.ShapeDtypeStruct((TOP_K, t), I32), jax.ShapeDtypeStruct((TOP_K, t), F32),
                   jax.ShapeDtypeStruct((1, nbp), I32), jax.ShapeDtypeStruct((1, LANES), I32)],
        scratch_shapes=[pltpu.VMEM((TOP_K, t), F32), pltpu.VMEM((TOP_K, t), F32), pltpu.VMEM((e, LANES), F32)],
        compiler_params=pltpu.CompilerParams(vmem_limit_bytes=32 * 1024 * 1024),
        name="moe_routing",
    )(logits_t)


def _sc_mesh():
    return plsc.VectorSubcoreMesh(core_axis_name="c", subcore_axis_name="s",
                                  num_cores=V7X_SC_CORES, num_subcores=V7X_SC_SUBCORES)


def _sc_worker_id():
    return lax.axis_index("s") * V7X_SC_CORES + lax.axis_index("c")


def _dispatch_sc(dest_chunks, hp, *, n_pad):
    t, c = hp.shape
    per_worker = t // SC_CHUNK // (V7X_SC_CORES * V7X_SC_SUBCORES)

    @functools.partial(
        pl.kernel, mesh=_sc_mesh(), out_type=jax.ShapeDtypeStruct((n_pad, c), I32),
        scratch_types=[pltpu.VMEM((TOP_K, SC_CHUNK), I32), pltpu.VMEM((SC_CHUNK, c), I32), pltpu.SemaphoreType.DMA],
        name="moe_dispatch_sc")
    def scatter_rows(dest_hbm, hp_hbm, xs_hbm, idx_v, rows_v, sem):
        wid = _sc_worker_id()

        @pl.loop(0, per_worker)
        def _(i):
            chunk = wid * per_worker + i
            pltpu.sync_copy(dest_hbm.at[chunk], idx_v)
            pltpu.sync_copy(hp_hbm.at[pl.ds(chunk * SC_CHUNK, SC_CHUNK)], rows_v)
            copies = [pltpu.async_copy(rows_v, xs_hbm.at[idx_v.at[k]], sem) for k in range(TOP_K)]
            for cp in copies:
                cp.wait()

    return scatter_rows(dest_chunks, hp)


def _gather_sc(dest_chunks, ys, *, t):
    c = ys.shape[1]
    per_worker = t // SC_CHUNK // (V7X_SC_CORES * V7X_SC_SUBCORES)

    @functools.partial(
        pl.kernel, mesh=_sc_mesh(), out_type=jax.ShapeDtypeStruct((TOP_K, t, c), I32),
        scratch_types=[pltpu.VMEM((TOP_K, SC_CHUNK), I32), pltpu.VMEM((SC_CHUNK, c), I32), pltpu.SemaphoreType.DMA],
        name="moe_gather_sc")
    def gather_rows(dest_hbm, ys_hbm, out_hbm, idx_v, rows_v, sem):
        wid = _sc_worker_id()

        @pl.loop(0, per_worker)
        def _(i):
            chunk = wid * per_worker + i
            pltpu.sync_copy(dest_hbm.at[chunk], idx_v)
            for k in range(TOP_K):
                pltpu.async_copy(ys_hbm.at[idx_v.at[k]], rows_v, sem).wait()
                pltpu.sync_copy(rows_v, out_hbm.at[k, pl.ds(chunk * SC_CHUNK, SC_CHUNK)])

    return gather_rows(dest_chunks, ys)


def _expert_kernel(blke_ref, nused_ref, xs_ref, w1_ref, b1_ref, w2_ref, b2_ref, ys_ref, w1b_ref, w2b_ref):
    j = pl.program_id(0)

    @pl.when((j == 0) | (blke_ref[j] != blke_ref[jnp.maximum(j - 1, 0)]))
    def _():
        w1b_ref[...] = w1_ref[0].astype(BF16)
        w2b_ref[...] = w2_ref[0].astype(BF16)

    @pl.when(j < nused_ref[0])
    def _():
        f = w2_ref.shape[1]
        x = _unpack_bf16_pairs(xs_ref[...]).astype(BF16)
        hid = _dot(x, w1b_ref[...]) + b1_ref[0]
        gate = jnp.minimum(hid[:, :f], SWIGLU_LIMIT)
        lin = jnp.clip(hid[:, f:], -SWIGLU_LIMIT, SWIGLU_LIMIT)
        act = gate * jax.nn.sigmoid(SWIGLU_ALPHA * gate) * (lin + 1.0)
        y = _dot(act.astype(BF16), w2b_ref[...]) + b2_ref[0]
        ys_ref[...] = _pack_bf16_pairs(y)

    @pl.when(pl.program_id(0) >= nused_ref[0])
    def _():
        ys_ref[...] = jnp.zeros_like(ys_ref)


def _experts(blk_e, n_used, xs, w1, b1, w2, b2, *, rows):
    n_pad, c = xs.shape
    e, d, f2 = w1.shape
    f = f2 // 2
    n_blk = n_pad // rows
    row_blk = lambda j, be, nu: (jnp.minimum(j, nu[0] - 1), 0)
    wsel = lambda j, be, nu: (be[j], 0, 0)
    grid_spec = pltpu.PrefetchScalarGridSpec(
        num_scalar_prefetch=2,
        grid=(n_blk,),
        in_specs=[pl.BlockSpec((rows, c), row_blk),
                  pl.BlockSpec((1, d, f2), wsel), pl.BlockSpec((1, 1, f2), wsel),
                  pl.BlockSpec((1, f, d), wsel), pl.BlockSpec((1, 1, d), wsel)],
        out_specs=pl.BlockSpec((rows, c), lambda j, be, nu: (j, 0)),
        scratch_shapes=[pltpu.VMEM((d, f2), BF16), pltpu.VMEM((f, d), BF16)],
    )
    return pl.pallas_call(
        _expert_kernel,
        grid_spec=grid_spec,
        out_shape=jax.ShapeDtypeStruct((n_pad, c), I32),
        compiler_params=_cparams(("arbitrary",), 60),
        name="moe_experts",
    )(blk_e, n_used, xs, w1, b1, w2, b2)


def _combine_kernel(h_ref, gate_ref, g_ref, rows_ref, o_ref, *, final_norm):
    gates = gate_ref[...]
    out = h_ref[...]
    for k in range(TOP_K):
        out = out + _unpack_bf16_pairs(rows_ref[k]) * gates[:, k:k + 1]
    if final_norm:
        out = _rms(out, g_ref[...])
    o_ref[...] = out


def _combine(h, gates_t, g_final, rows, *, tm, final_norm):
    t, d = h.shape
    c = rows.shape[2]
    kern = functools.partial(_combine_kernel, final_norm=final_norm)
    return pl.pallas_call(
        kern,
        grid=(t // tm,),
        in_specs=[pl.BlockSpec((tm, d), lambda i: (i, 0)),
                  pl.BlockSpec((tm, TOP_K), lambda i: (i, 0)),
                  pl.BlockSpec((1, d), lambda i: (0, 0)),
                  pl.BlockSpec((TOP_K, tm, c), lambda i: (0, i, 0))],
        out_specs=pl.BlockSpec((tm, d), lambda i: (i, 0)),
        out_shape=jax.ShapeDtypeStruct((t, d), F32),
        compiler_params=_cparams(("parallel",), 32),
        name="moe_combine",
    )(h, gates_t, g_final, rows)


def _pick_tile(n, pref):
    t = min(n, pref)
    while n % t:
        t //= 2
    return t


def kernel(x, mem, norm_mix_g, w_in, lambda_q1, lambda_k1, lambda_q2, lambda_k2, subln_g, ssm_a_re, ssm_a_im, ssm_log_dt, ssm_b_re, ssm_b_im, ssm_c_re, ssm_c_im, ssm_d, w_glu, w_branch_attn, w_branch_ssm, w_mix_out, norm_cross_g, norm_mem_g, w_xq, w_xkv, w_xo, norm_ffn_g, w_router, b_router, w_e1, b_e1, w_e2, b_e2, norm_final_g):
    b, l, d = x.shape
    depth = w_in.shape[0]
    head_dim = lambda_q1.shape[-1]
    da_width = w_branch_attn.shape[1]
    heads = da_width // (2 * head_dim)
    ssm_width = w_branch_ssm.shape[1]
    n_exp = w_router.shape[-1]
    t = b * l
    assert d % (2 * LANES) == 0 and da_width % LANES == 0 and ssm_width % LANES == 0 and LANES % head_dim == 0
    assert head_dim & (head_dim - 1) == 0

    tl = _pick_tile(l, 512)
    tq = _pick_tile(l, 512)
    tc = _pick_tile(l, 64)
    tm = _pick_tile(t, 256)
    tr = _pick_tile(t, 512)
    rows = MOE_ROWS
    n_blk = (t * TOP_K) // rows + n_exp
    n_pad = n_blk * rows

    half = head_dim // 2
    inv = ROPE_THETA ** (-jnp.arange(half, dtype=F32) * (2.0 / head_dim))
    ang = jnp.arange(l, dtype=F32)[:, None] * inv[None, :]
    cos_t = jnp.tile(jnp.concatenate([jnp.cos(ang), jnp.cos(ang)], axis=1), (1, da_width // head_dim))
    sin_t = jnp.tile(jnp.concatenate([-jnp.sin(ang), jnp.sin(ang)], axis=1), (1, da_width // head_dim))

    row = lambda v: v.astype(F32).reshape(1, -1)
    h = x.astype(F32)
    out = None
    for layer in range(depth):
        lambda_init = 0.8 - 0.6 * math.exp(-0.3 * layer)
        w_vt = w_in[layer][:, 2 * da_width:3 * da_width].T.astype(BF16)
        q, k, vt, u2, g_a, g_s = _inproj(h, row(norm_mix_g[layer]), w_in[layer].astype(BF16), w_vt, cos_t, sin_t,
                                         da_width=da_width, ssm_width=ssm_width, head_dim=head_dim, tl=tl)
        lam_p = jnp.stack([lambda_q1[layer], lambda_k1[layer], lambda_q2[layer], lambda_k2[layer]]).astype(F32)
        o_a = _diffattn(lam_p, q, k, vt, row(subln_g[layer]), heads=heads, head_dim=head_dim,
                        lambda_init=lambda_init, tq=tq)

        ab_re, ab_im, bb_re, bb_im = _s5_discretize(ssm_a_re[layer], ssm_a_im[layer], ssm_log_dt[layer],
                                                    ssm_b_re[layer], ssm_b_im[layer])
        o_s = _s5_mixer(u2.reshape(l, b, ssm_width), ab_re, ab_im, bb_re, bb_im,
                        ssm_c_re[layer].astype(F32), ssm_c_im[layer].astype(F32),
                        row(ssm_d[layer]), w_glu[layer].astype(BF16), tc=tc)

        kx, vx = _memkv(mem.astype(F32), row(norm_mem_g[layer]), w_xkv[layer].astype(BF16))
        wr_t = w_router[layer].astype(F32).T
        wr_hi = wr_t.astype(BF16)
        wr_lo = (wr_t - wr_hi.astype(F32)).astype(BF16)
        h2, hp, logits_t = _mix_cross(
            h, o_a, o_s.reshape(l, b * ssm_width), g_a, g_s,
            w_branch_attn[layer].astype(BF16), w_branch_ssm[layer].astype(BF16), w_mix_out[layer].astype(BF16),
            row(norm_cross_g[layer]), w_xq[layer].astype(BF16), kx, vx, w_xo[layer].astype(BF16),
            row(norm_ffn_g[layer]), wr_hi, wr_lo, b_router[layer].astype(F32).reshape(n_exp, 1), tl=tl)

        dest, gates, blk_e, n_used = _routing(logits_t, rows=rows, n_blk=n_blk, tr=tr)
        dest_chunks = dest.reshape(TOP_K, t // SC_CHUNK, SC_CHUNK).transpose(1, 0, 2)
        xs = _dispatch_sc(dest_chunks, hp, n_pad=n_pad)
        ys = _experts(blk_e[0, :n_blk], n_used[0, :1], xs,
                      w_e1[layer].astype(F32), b_e1[layer].astype(F32)[:, None, :],
                      w_e2[layer].astype(F32), b_e2[layer].astype(F32)[:, None, :], rows=rows)
        last = layer == depth - 1
        rows_k = _gather_sc(dest_chunks, ys, t=t)
        h_flat = _combine(h2.reshape(t, d), gates.T, row(norm_final_g), rows_k, tm=tm, final_norm=last)
        h = h_flat.reshape(b, l, d)
        out = h
    return out.astype(x.dtype)
```
